```python
import math
import jax, jax.numpy as jnp
from jax import lax
import numpy as np

D_MODEL = 2048
BATCH = 4
SEQ = 4096
DEPTH = 4
DEC_BATCH = 4
DEC_SEQ = 8192
PAST_LEN = 128

N_EVEN = (DEPTH + 1) // 2
N_ODD = DEPTH // 2
H_A = 16
DH_A = 64
W_A = H_A * DH_A
DILATED_PATTERNS = ((128, 1), (512, 4), (2048, 16))
D_RNN = 1024
LRU_BLOCKS = 8
LRU_BW = D_RNN // LRU_BLOCKS
CONV_W = 4
LRU_C = 8.0
AB_IN = 3 * W_A + 2 * D_RNN
AB_MIX = W_A + D_RNN
H_C = 8
DK_C = 128
DV_C = 256
C_IN = H_C * (4 * DK_C + DV_C)
C_MIX = H_C * DV_C
Q_BLOCK = 128
NUM_BUCKETS = 32
MAX_DISTANCE = 2048
H_X = 4
DH_X = 128
N_MEM = 256
N_GROUPS = 4
EXPERTS_PER_GROUP = 4
N_EXPERTS = N_GROUPS * EXPERTS_PER_GROUP
TOP_K_IN_GROUP = 2
D_FF_EXPERT = 1024
MOE_BLOCK = 256
EPS = 1e-6
NEG = -1e30

kernel_name = "hybrid_dilated_lru_diffattn_hmoe_encoder"


def _rmsnorm(x, g):
    x32 = x.astype(jnp.float32)
    y = x32 * lax.rsqrt(jnp.mean(x32 * x32, axis=-1, keepdims=True) + EPS)
    return (y * g.astype(jnp.float32)).astype(x.dtype)


def _t5_bucket(rel):
    half = NUM_BUCKETS // 2
    exact = half // 2
    n = jnp.abs(rel)
    large = exact + (jnp.log(jnp.maximum(n, 1).astype(jnp.float32) / exact)
                     / math.log(MAX_DISTANCE / exact) * (half - exact)).astype(jnp.int32)
    large = jnp.minimum(large, half - 1)
    return (rel > 0).astype(jnp.int32) * half + jnp.where(n < exact, n, large)


def _banded_attention(q, k, v, radius, dilation, table):
    N, L, H, dh = q.shape
    f32 = jnp.float32
    W = radius
    nb = -(-L // W)
    Lp = nb * W
    qb = jnp.pad(q, ((0, 0), (0, Lp - L), (0, 0), (0, 0))).reshape(N, nb, W, H, dh)
    pad_kv = ((0, 0), (W, Lp - L + W), (0, 0), (0, 0))
    kp = jnp.pad(k, pad_kv)
    vp = jnp.pad(v, pad_kv)

    def _blocks(t):
        return jnp.concatenate([t[:, i * W:i * W + Lp].reshape(N, nb, W, H, dh) for i in range(3)], axis=2)

    kb = _blocks(kp)
    vb = _blocks(vp)
    rel = jnp.arange(3 * W, dtype=jnp.int32)[None, :] - W - jnp.arange(W, dtype=jnp.int32)[:, None]
    key_pos = (jnp.arange(nb, dtype=jnp.int32)[:, None, None] * W
               + jnp.arange(W, dtype=jnp.int32)[None, :, None] + rel[None])
    mask = (jnp.abs(rel) <= radius)[None] & (key_pos >= 0) & (key_pos < L)
    bias = table[_t5_bucket(rel * dilation)].astype(f32).transpose(2, 0, 1)
    s = jnp.einsum('njqhd,njkhd->njhqk', qb, kb, preferred_element_type=f32) * (dh ** -0.5) + bias
    s = jnp.where(mask[None, :, None], s, NEG)
    m = jnp.max(s, axis=-1, keepdims=True)
    p = jnp.exp(s - m)
    l = jnp.sum(p, axis=-1, keepdims=True)
    o = jnp.einsum('njhqk,njkhd->njqhd', p / l, vb.astype(f32))
    lse = (m + jnp.log(l))[..., 0]
    o = o.reshape(N, Lp, H, dh)[:, :L]
    lse = lse.transpose(0, 1, 3, 2).reshape(N, Lp, H)[:, :L]
    return o, lse


def _dilated_attention(q, k, v, table):
    B, S, H, dh = q.shape
    outs, lses = [], []
    for window, dil in DILATED_PATTERNS:
        radius = (window // 2) // dil
        sub = lambda t, d=dil: t.reshape(B, S // d, d, H, dh).swapaxes(1, 2).reshape(B * d, S // d, H, dh)
        o, lse = _banded_attention(sub(q), sub(k), sub(v), radius, dil, table)
        outs.append(o.reshape(B, dil, S // dil, H, dh).swapaxes(1, 2).reshape(B, S, H, dh))
        lses.append(lse.reshape(B, dil, S // dil, H).swapaxes(1, 2).reshape(B, S, H))
    w = jax.nn.softmax(jnp.stack(lses), axis=0)
    return jnp.einsum('gbsh,gbshd->bshd', w, jnp.stack(outs))


def _centred_conv(x, w, b):
    S = x.shape[1]
    left = (CONV_W - 1) // 2
    xp = jnp.pad(x, ((0, 0), (left, CONV_W - 1 - left), (0, 0)))
    y = b
    for t in range(CONV_W):
        y = y + xp[:, t:t + S] * w[t]
    return y


def _rg_lru(x, wa, ba, wx, bx, lam, reverse):
    B, S, C = x.shape
    xg = x.reshape(B, S, LRU_BLOCKS, LRU_BW)
    r = jax.nn.sigmoid(jnp.einsum('bsnc,ncd->bsnd', xg, wa).reshape(B, S, C) + ba)
    i = jax.nn.sigmoid(jnp.einsum('bsnc,ncd->bsnd', xg, wx).reshape(B, S, C) + bx)
    log_a = -LRU_C * jax.nn.softplus(-lam) * r
    a = jnp.exp(log_a)
    u = jnp.sqrt(-jnp.expm1(2.0 * log_a)) * (i * x)

    def comb(lhs, rhs):
        a1, b1 = lhs
        a2, b2 = rhs
        return a1 * a2, a2 * b1 + b2

    _, h = lax.associative_scan(comb, (a, u), reverse=reverse, axis=1)
    return h


def _mixer_ab(h, w_in, conv_w, conv_b, lru_wa, lru_ba, lru_wx, lru_bx, lru_lam, w_out, table):
    B, S, _ = h.shape
    f32 = jnp.float32
    z = h @ w_in
    q, k, v, xr, gr = jnp.split(z, [W_A, 2 * W_A, 3 * W_A, 3 * W_A + D_RNN], axis=-1)
    hd = lambda t: t.reshape(B, S, H_A, DH_A)
    attn = _dilated_attention(hd(q), hd(k), hd(v), table).reshape(B, S, W_A)
    xc = _centred_conv(xr.astype(f32), conv_w.astype(f32), conv_b.astype(f32))
    rec = (_rg_lru(xc, lru_wa[0], lru_ba[0], lru_wx[0], lru_bx[0], lru_lam[0], False)
           + _rg_lru(xc, lru_wa[1], lru_ba[1], lru_wx[1], lru_bx[1], lru_lam[1], True))
    rec = jax.nn.gelu(gr.astype(f32)) * rec
    mix = jnp.concatenate([attn.astype(h.dtype), rec.astype(h.dtype)], axis=-1)
    return mix @ w_out


def _diff_attention(h, w_in, lam_vecs, subln, w_out, table, lambda_init):
    B, S, _ = h.shape
    f32 = jnp.float32
    z = h @ w_in
    qk_w = H_C * 2 * DK_C
    q = z[..., :qk_w].reshape(B, S, H_C, 2, DK_C)
    k = z[..., qk_w:2 * qk_w].reshape(B, S, H_C, 2, DK_C)
    v = z[..., 2 * qk_w:].reshape(B, S, H_C, DV_C).astype(f32)
    lv = lam_vecs.astype(f32)
    lam = jnp.exp(jnp.sum(lv[0] * lv[1])) - jnp.exp(jnp.sum(lv[2] * lv[3])) + lambda_init
    nq = S // Q_BLOCK
    qb = q.reshape(B, nq, Q_BLOCK, H_C, 2, DK_C).swapaxes(0, 1)
    starts = jnp.arange(nq, dtype=jnp.int32) * Q_BLOCK
    kpos = jnp.arange(S, dtype=jnp.int32)
    scale = DK_C ** -0.5

    def block(args):
        qblk, q0 = args
        s = jnp.einsum('bqhmd,bkhmd->bhmqk', qblk, k, preferred_element_type=f32) * scale
        rel = kpos[None, :] - (q0 + jnp.arange(Q_BLOCK, dtype=jnp.int32))[:, None]
        bias = table[_t5_bucket(rel)].astype(f32).transpose(2, 0, 1)
        p = jax.nn.softmax(s + bias[None, :, None], axis=-1)
        attn = p[:, :, 0] - lam * p[:, :, 1]
        return jnp.einsum('bhqk,bkhd->bqhd', attn, v)

    o = lax.map(block, (qb, starts)).swapaxes(0, 1).reshape(B, S, H_C, DV_C)
    o = _rmsnorm(o, subln) * (1.0 - lambda_init)
    return o.reshape(B, S, C_MIX).astype(h.dtype) @ w_out


def _cross_attention(h, mem_n, wq, wkv, wo):
    B, S, _ = h.shape
    M = mem_n.shape[1]
    f32 = jnp.float32
    q = (h @ wq).reshape(B, S, H_X, DH_X)
    kv = (mem_n @ wkv).reshape(B, M, 2, H_X, DH_X)
    s = jnp.einsum('bqhd,bkhd->bhqk', q, kv[:, :, 0], preferred_element_type=f32) * (DH_X ** -0.5)
    p = jax.nn.softmax(s, axis=-1)
    o = jnp.einsum('bhqk,bkhd->bqhd', p, kv[:, :, 1].astype(f32))
    return o.reshape(B, S, H_X * DH_X).astype(h.dtype) @ wo


def _hier_moe(h, wg, bg, we, be, w_gate, w_up, w_down):
    B, S, D = h.shape
    T = B * S
    f32 = jnp.float32
    hf = h.reshape(T, D)
    g_logits = jnp.dot(hf, wg, preferred_element_type=f32) + bg
    g_idx = jnp.argmax(g_logits, axis=-1)
    p_group = jnp.take_along_axis(jax.nn.softmax(g_logits, axis=-1), g_idx[:, None], axis=1)[:, 0]
    e_logits = (jnp.dot(hf, we, preferred_element_type=f32) + be).reshape(T, N_GROUPS, EXPERTS_PER_GROUP)
    e_logits = jnp.take_along_axis(e_logits, g_idx[:, None, None], axis=1)[:, 0]
    top_v, top_i = lax.top_k(e_logits, TOP_K_IN_GROUP)
    top_w = jax.nn.softmax(top_v, axis=-1) * p_group[:, None]
    flat_e = (g_idx[:, None] * EXPERTS_PER_GROUP + top_i).reshape(-1).astype(jnp.int32)
    flat_w = top_w.reshape(-1)
    flat_t = jnp.repeat(jnp.arange(T, dtype=jnp.int32), TOP_K_IN_GROUP)
    n_rows = T * TOP_K_IN_GROUP
    order = jnp.argsort(flat_e)
    sorted_e = flat_e[order]
    counts = jnp.bincount(flat_e, length=N_EXPERTS).astype(jnp.int32)
    padded = (counts + MOE_BLOCK - 1) // MOE_BLOCK * MOE_BLOCK
    pad_end = jnp.cumsum(padded)
    pad_start = pad_end - padded
    start = jnp.cumsum(counts) - counts
    dest = pad_start[sorted_e] + jnp.arange(n_rows, dtype=jnp.int32) - start[sorted_e]
    cap = (-(-n_rows // MOE_BLOCK) + N_EXPERTS) * MOE_BLOCK
    buf_tok = jnp.zeros((cap,), jnp.int32).at[dest].set(flat_t[order])
    buf_w = jnp.zeros((cap,), f32).at[dest].set(flat_w[order])
    n_blk = cap // MOE_BLOCK
    blk_e = jnp.minimum(jnp.searchsorted(pad_end, jnp.arange(n_blk, dtype=jnp.int32) * MOE_BLOCK, side='right'),
                        N_EXPERTS - 1)
    xb = hf[buf_tok].reshape(n_blk, MOE_BLOCK, D)

    def expert_block(args):
        xblk, e = args
        hid = jax.nn.silu(xblk @ w_gate[e]) * (xblk @ w_up[e])
        return hid @ w_down[e]

    out = lax.map(expert_block, (xb, blk_e)).reshape(cap, D)
    y = jnp.zeros((T, D), f32).at[buf_tok].add(out.astype(f32) * buf_w[:, None])
    return y.reshape(B, S, D).astype(h.dtype)


def _lambda_init(layer):
    return 0.8 - 0.6 * math.exp(-0.3 * layer)


def _trunk(x, mem, rel_bias, ab_w_in, ab_conv_w, ab_conv_b, lru_wa, lru_ba, lru_wx, lru_bx, lru_lam,
           ab_w_out, c_w_in, c_lam, c_subln, c_w_out, norm_mix, norm_cross, norm_mem, x_wq, x_wkv, x_wo,
           norm_ffn, moe_wg, moe_bg, moe_we, moe_be, moe_w_gate, moe_w_up, moe_w_down, norm_final):
    for l in range(DEPTH):
        h = _rmsnorm(x, norm_mix[l])
        i = l // 2
        if l % 2 == 0:
            x = x + _mixer_ab(h, ab_w_in[i], ab_conv_w[i], ab_conv_b[i], lru_wa[i], lru_ba[i], lru_wx[i],
                              lru_bx[i], lru_lam[i], ab_w_out[i], rel_bias[:, :H_A])
        else:
            x = x + _diff_attention(h, c_w_in[i], c_lam[i], c_subln[i], c_w_out[i], rel_bias[:, H_A:],
                                    _lambda_init(l))
        h = _rmsnorm(x, norm_cross[l])
        x = x + _cross_attention(h, _rmsnorm(mem, norm_mem[l]), x_wq[l], x_wkv[l], x_wo[l])
        h = _rmsnorm(x, norm_ffn[l])
        x = x + _hier_moe(h, moe_wg[l], moe_bg[l], moe_we[l], moe_be[l], moe_w_gate[l], moe_w_up[l],
                          moe_w_down[l])
    return _rmsnorm(x, norm_final)


def setup_inputs(seed: int = 0) -> dict:
    key = jax.random.key(seed)
    ks = iter(jax.random.split(key, 40))
    f32 = jnp.float32

    def nrm(shape, scale):
        return jax.random.normal(next(ks), shape, f32) * scale

    def gain(shape):
        return 1.0 + 0.02 * jax.random.normal(next(ks), shape, f32)

    a0 = jax.random.uniform(next(ks), (N_EVEN, 2, D_RNN), f32, minval=0.9, maxval=0.999)
    s0 = a0 ** (1.0 / LRU_C)
    lru_lam = jnp.log(s0) - jnp.log1p(-s0)
    return {
        "x_prompt": nrm((BATCH, SEQ, D_MODEL), 1.0),
        "x_sample": nrm((DEC_BATCH, DEC_SEQ, D_MODEL), 1.0),
        "mem_prompt": nrm((BATCH, N_MEM, D_MODEL), 1.0),
        "mem_sample": nrm((DEC_BATCH, N_MEM, D_MODEL), 1.0),
        "rel_bias": nrm((NUM_BUCKETS, H_A + H_C), 0.3),
        "ab_w_in": nrm((N_EVEN, D_MODEL, AB_IN), D_MODEL ** -0.5),
        "ab_conv_w": nrm((N_EVEN, CONV_W, D_RNN), CONV_W ** -0.5),
        "ab_conv_b": nrm((N_EVEN, D_RNN), 0.02),
        "lru_wa": nrm((N_EVEN, 2, LRU_BLOCKS, LRU_BW, LRU_BW), LRU_BW ** -0.5),
        "lru_ba": nrm((N_EVEN, 2, D_RNN), 0.02),
        "lru_wx": nrm((N_EVEN, 2, LRU_BLOCKS, LRU_BW, LRU_BW), LRU_BW ** -0.5),
        "lru_bx": nrm((N_EVEN, 2, D_RNN), 0.02),
        "lru_lam": lru_lam,
        "ab_w_out": nrm((N_EVEN, AB_MIX, D_MODEL), AB_MIX ** -0.5),
        "c_w_in": nrm((N_ODD, D_MODEL, C_IN), D_MODEL ** -0.5),
        "c_lam": nrm((N_ODD, 4, DK_C), 0.1),
        "c_subln": gain((N_ODD, DV_C)),
        "c_w_out": nrm((N_ODD, C_MIX, D_MODEL), C_MIX ** -0.5),
        "norm_mix": gain((DEPTH, D_MODEL)),
        "norm_cross": gain((DEPTH, D_MODEL)),
        "norm_mem": gain((DEPTH, D_MODEL)),
        "x_wq": nrm((DEPTH, D_MODEL, H_X * DH_X), D_MODEL ** -0.5),
        "x_wkv": nrm((DEPTH, D_MODEL, 2 * H_X * DH_X), D_MODEL ** -0.5),
        "x_wo": nrm((DEPTH, H_X * DH_X, D_MODEL), (H_X * DH_X) ** -0.5),
        "norm_ffn": gain((DEPTH, D_MODEL)),
        "moe_wg": nrm((DEPTH, D_MODEL, N_GROUPS), D_MODEL ** -0.5),
        "moe_bg": nrm((DEPTH, N_GROUPS), 0.01),
        "moe_we": nrm((DEPTH, D_MODEL, N_EXPERTS), D_MODEL ** -0.5),
        "moe_be": nrm((DEPTH, N_EXPERTS), 0.01),
        "moe_w_gate": nrm((DEPTH, N_EXPERTS, D_MODEL, D_FF_EXPERT), D_MODEL ** -0.5),
        "moe_w_up": nrm((DEPTH, N_EXPERTS, D_MODEL, D_FF_EXPERT), D_MODEL ** -0.5),
        "moe_w_down": nrm((DEPTH, N_EXPERTS, D_FF_EXPERT, D_MODEL), D_FF_EXPERT ** -0.5),
        "norm_final": gain((D_MODEL,)),
    }


def reference(x_prompt, x_sample, mem_prompt, mem_sample, rel_bias, ab_w_in, ab_conv_w, ab_conv_b,
              lru_wa, lru_ba, lru_wx, lru_bx, lru_lam, ab_w_out, c_w_in, c_lam, c_subln, c_w_out,
              norm_mix, norm_cross, norm_mem, x_wq, x_wkv, x_wo, norm_ffn, moe_wg, moe_bg, moe_we, moe_be,
              moe_w_gate, moe_w_up, moe_w_down, norm_final):
    y_prompt = _trunk(x_prompt, mem_prompt, rel_bias, ab_w_in, ab_conv_w, ab_conv_b, lru_wa, lru_ba,
                      lru_wx, lru_bx, lru_lam, ab_w_out, c_w_in, c_lam, c_subln, c_w_out, norm_mix,
                      norm_cross, norm_mem, x_wq, x_wkv, x_wo, norm_ffn, moe_wg, moe_bg, moe_we, moe_be,
                      moe_w_gate, moe_w_up, moe_w_down, norm_final)
    y_sample = _trunk(x_sample, mem_sample, rel_bias, ab_w_in, ab_conv_w, ab_conv_b, lru_wa, lru_ba,
                      lru_wx, lru_bx, lru_lam, ab_w_out, c_w_in, c_lam, c_subln, c_w_out, norm_mix,
                      norm_cross, norm_mem, x_wq, x_wkv, x_wo, norm_ffn, moe_wg, moe_bg, moe_we, moe_be,
                      moe_w_gate, moe_w_up, moe_w_down, norm_final)
    return (y_prompt, y_sample)
```

```python
import functools
import math

import jax
import jax.numpy as jnp
from jax import lax
from jax.experimental import pallas as pl
from jax.experimental.pallas import tpu as pltpu

F32 = jnp.float32
BF16 = jnp.bfloat16

D_MODEL = 2048
DEPTH = 4
H_A, DH_A = 16, 64
W_A = H_A * DH_A
DILATED_PATTERNS = ((128, 1), (512, 4), (2048, 16))
D_RNN = 1024
LRU_BLOCKS, LRU_BW = 8, 128
CONV_W = 4
LRU_C = 8.0
H_C, DK_C, DV_C = 8, 128, 256
NUM_BUCKETS, MAX_DISTANCE = 32, 2048
H_X, DH_X, N_MEM = 4, 128, 256
N_GROUPS, EXPERTS_PER_GROUP = 4, 4
N_EXPERTS = N_GROUPS * EXPERTS_PER_GROUP
D_FF = 1024
EPS = 1e-6
NEG = -1e30

LANES = 128
VMEM_LIMIT = 56 * 1024 * 1024


def _params(sem):
    return pltpu.CompilerParams(dimension_semantics=sem, vmem_limit_bytes=VMEM_LIMIT)


def _rms(x, g):
    return x * lax.rsqrt(jnp.mean(x * x, axis=-1, keepdims=True) + EPS) * g


def _norm_matmul_kernel(x_ref, g_ref, w_ref, o_ref, h_ref):
    @pl.when(pl.program_id(1) == 0)
    def _():
        h_ref[...] = _rms(x_ref[...], g_ref[...]).astype(BF16)

    o_ref[...] = jnp.dot(h_ref[...], w_ref[...], preferred_element_type=F32).astype(o_ref.dtype)


def norm_matmul(x, g, w, out_dtype, tm, tn):
    T, D = x.shape
    N = w.shape[1]
    tm, tn = min(tm, T), min(tn, N)
    assert T % tm == 0 and N % tn == 0
    return pl.pallas_call(
        _norm_matmul_kernel,
        grid=(T // tm, N // tn),
        in_specs=[pl.BlockSpec((tm, D), lambda i, j: (i, 0)),
                  pl.BlockSpec((1, D), lambda i, j: (0, 0)),
                  pl.BlockSpec((D, tn), lambda i, j: (0, j))],
        out_specs=pl.BlockSpec((tm, tn), lambda i, j: (i, j)),
        out_shape=jax.ShapeDtypeStruct((T, N), out_dtype),
        scratch_shapes=[pltpu.VMEM((tm, D), BF16)],
        compiler_params=_params(("parallel", "arbitrary")),
        name="norm_matmul",
    )(x, g.reshape(1, D), w)


def _mix_out_kernel(a1_ref, a2_ref, w_ref, x_ref, o_ref):
    k1 = a1_ref.shape[1]
    acc = jnp.dot(a1_ref[...], w_ref[0:k1, :], preferred_element_type=F32)
    acc += jnp.dot(a2_ref[...], w_ref[k1:, :], preferred_element_type=F32)
    o_ref[...] = x_ref[...] + acc


def mix_out(a1, a2, w, x, tm):
    T, D = x.shape
    K1, K2 = a1.shape[1], a2.shape[1]
    tm = min(tm, T)
    assert T % tm == 0
    return pl.pallas_call(
        _mix_out_kernel,
        grid=(T // tm,),
        in_specs=[pl.BlockSpec((tm, K1), lambda i: (i, 0)),
                  pl.BlockSpec((tm, K2), lambda i: (i, 0)),
                  pl.BlockSpec((K1 + K2, D), lambda i: (0, 0)),
                  pl.BlockSpec((tm, D), lambda i: (i, 0))],
        out_specs=pl.BlockSpec((tm, D), lambda i: (i, 0)),
        out_shape=jax.ShapeDtypeStruct((T, D), F32),
        compiler_params=_params(("parallel",)),
        name="mix_out",
    )(a1, a2, w, x)


def _t5_bucket(rel):
    half = NUM_BUCKETS // 2
    exact = half // 2
    n = jnp.abs(rel)
    large = exact + (jnp.log(jnp.maximum(n, 1).astype(F32) / exact)
                     / math.log(MAX_DISTANCE / exact) * (half - exact)).astype(jnp.int32)
    large = jnp.minimum(large, half - 1)
    return (rel > 0).astype(jnp.int32) * half + jnp.where(n < exact, n, large)


DIL_RADIUS = 64


def _dilated_bias(table, dil, tq):
    win = tq + 2 * DIL_RADIUS
    r = jnp.arange(tq, dtype=jnp.int32)[:, None]
    c = jnp.arange(win, dtype=jnp.int32)[None, :]
    tiles = []
    for off in (0, -DIL_RADIUS, -2 * DIL_RADIUS):
        rel = off + c - r
        b = table[_t5_bucket(rel * dil)].astype(F32).transpose(2, 0, 1)
        tiles.append(jnp.where((jnp.abs(rel) <= DIL_RADIUS)[None], b, NEG))
    return jnp.stack(tiles)


def _dilated_kernel(q_ref, k_ref, v_ref, b_ref, o_ref, lse_ref, *, L, tq):
    nq = L // tq
    win = tq + 2 * DIL_RADIUS
    head0 = lax.broadcasted_iota(jnp.int32, (1, LANES), 1) < DH_A

    def body(i, carry):
        i0 = pl.multiple_of(i * tq, tq)
        start = pl.multiple_of(jnp.clip(i0 - DIL_RADIUS, 0, L - win), DIL_RADIUS)
        var = jnp.where(i == 0, 0, jnp.where(i == nq - 1, 2, 1))
        q = q_ref[0, pl.ds(i0, tq), :]
        k = k_ref[0, pl.ds(start, win), :]
        v = v_ref[0, pl.ds(start, win), :]
        outs, lses = [], []
        for h in range(2):
            qh = jnp.where(head0 if h == 0 else jnp.logical_not(head0), q, jnp.zeros_like(q))
            s = lax.dot_general(qh, k, (((1,), (1,)), ((), ())), preferred_element_type=F32)
            s = s + b_ref[var, h]
            m = jnp.max(s, axis=-1, keepdims=True)
            p = jnp.exp(s - m)
            l = jnp.sum(p, axis=-1, keepdims=True)
            outs.append(jnp.dot(p.astype(BF16), v, preferred_element_type=F32) / l)
            lses.append(m + jnp.log(l))
        o_ref[0, pl.ds(i0, tq), :] = jnp.where(head0, outs[0], outs[1]).astype(o_ref.dtype)
        lse_ref[0, pl.ds(i0, tq), :] = jnp.where(head0, lses[0], lses[1])
        return carry

    lax.fori_loop(0, nq, body, 0)


def dilated_attention(zqkv, table, B, S, dil, tq=128):
    L = S // dil
    assert S % dil == 0 and L % tq == 0 and L >= tq + 2 * DIL_RADIUS
    zv = zqkv.reshape(B, L, dil * 3 * W_A)
    nb = 3 * W_A // LANES
    hb = W_A // LANES
    bias = _dilated_bias(table, dil, tq)
    win = tq + 2 * DIL_RADIUS
    o, lse = pl.pallas_call(
        functools.partial(_dilated_kernel, L=L, tq=tq),
        grid=(B, dil, hb),
        in_specs=[pl.BlockSpec((1, L, LANES), lambda b, r, h: (b, 0, r * nb + h)),
                  pl.BlockSpec((1, L, LANES), lambda b, r, h: (b, 0, r * nb + hb + h)),
                  pl.BlockSpec((1, L, LANES), lambda b, r, h: (b, 0, r * nb + 2 * hb + h)),
                  pl.BlockSpec((3, 2, tq, win), lambda b, r, h: (0, h, 0, 0))],
        out_specs=[pl.BlockSpec((1, L, LANES), lambda b, r, h: (b, 0, r * hb + h)),
                   pl.BlockSpec((1, L, LANES), lambda b, r, h: (b, 0, r * hb + h))],
        out_shape=[jax.ShapeDtypeStruct((B, L, dil * W_A), BF16),
                   jax.ShapeDtypeStruct((B, L, dil * W_A), F32)],
        compiler_params=_params(("parallel", "parallel", "parallel")),
        name=f"dilated_attn_d{dil}",
    )(zv, zv, zv, bias)
    return o.reshape(B * S, W_A), lse.reshape(B * S, W_A)


def _merge_kernel(o1, o2, o3, l1, l2, l3, out_ref):
    a, b, c = l1[...], l2[...], l3[...]
    m = jnp.maximum(jnp.maximum(a, b), c)
    ea, eb, ec = jnp.exp(a - m), jnp.exp(b - m), jnp.exp(c - m)
    num = ea * o1[...].astype(F32) + eb * o2[...].astype(F32) + ec * o3[...].astype(F32)
    out_ref[...] = (num / (ea + eb + ec)).astype(out_ref.dtype)


def merge_patterns(os_, lses, tm):
    T, W = os_[0].shape
    tm = min(tm, T)
    spec = pl.BlockSpec((tm, W), lambda i: (i, 0))
    return pl.pallas_call(
        _merge_kernel,
        grid=(T // tm,),
        in_specs=[spec] * 6,
        out_specs=spec,
        out_shape=jax.ShapeDtypeStruct((T, W), BF16),
        compiler_params=_params(("parallel",)),
        name="merge_patterns",
    )(*os_, *lses)


def _lru_kernel(xr_ref, gr_ref, cw_ref, cb_ref, wa_ref, ba_ref, wx_ref, bx_ref, lc_ref, o_ref, hf_ref, *, S, C):
    nC = S // C
    row = lax.broadcasted_iota(jnp.int32, (C, LANES), 0)
    E = C + 16

    def conv_chunk(c):
        t0 = pl.multiple_of(c * C, C)
        cur = xr_ref[0, pl.ds(t0, C), :]
        pstart = pl.multiple_of(jnp.maximum(t0 - 8, 0), 8)
        nstart = pl.multiple_of(jnp.minimum(t0 + C, S - 8), 8)
        prev8 = jnp.where(c > 0, xr_ref[0, pl.ds(pstart, 8), :], 0.0)
        next8 = jnp.where(c < nC - 1, xr_ref[0, pl.ds(nstart, 8), :], 0.0)
        ext = jnp.concatenate([prev8, cur, next8], axis=0)
        xc = cb_ref[...] + ext[8:8 + C] * cw_ref[1:2, :]
        xc += pltpu.roll(ext, 1, 0)[8:8 + C] * cw_ref[0:1, :]
        xc += pltpu.roll(ext, E - 1, 0)[8:8 + C] * cw_ref[2:3, :]
        xc += pltpu.roll(ext, E - 2, 0)[8:8 + C] * cw_ref[3:4, :]
        return t0, xc

    def gates(xc, d):
        xb = xc.astype(BF16)
        r = jax.nn.sigmoid(jnp.dot(xb, wa_ref[d], preferred_element_type=F32) + ba_ref[d])
        i = jax.nn.sigmoid(jnp.dot(xb, wx_ref[d], preferred_element_type=F32) + bx_ref[d])
        log_a = lc_ref[d] * r
        a = jnp.exp(log_a)
        u = jnp.sqrt(-jnp.tanh(log_a) * (a * a + 1.0)) * (i * xc)
        return a, u

    def scan(a, u, reverse):
        k = 1
        while k < C:
            shift = C - k if reverse else k
            valid = (row < C - k) if reverse else (row >= k)
            u = u + a * jnp.where(valid, pltpu.roll(u, shift, 0), 0.0)
            a = a * jnp.where(valid, pltpu.roll(a, shift, 0), 1.0)
            k *= 2
        return a, u

    def fwd(c, carry):
        t0, xc = conv_chunk(c)
        a, u = scan(*gates(xc, 0), False)
        h = u + a * carry
        hf_ref[pl.ds(t0, C), :] = h
        return h[C - 1:C, :]

    lax.fori_loop(0, nC, fwd, jnp.zeros((1, LANES), F32))

    def bwd(j, carry):
        t0, xc = conv_chunk(nC - 1 - j)
        a, u = scan(*gates(xc, 1), True)
        h = u + a * carry
        g = jax.nn.gelu(gr_ref[0, pl.ds(t0, C), :])
        o_ref[0, pl.ds(t0, C), :] = (g * (hf_ref[pl.ds(t0, C), :] + h)).astype(o_ref.dtype)
        return h[0:1, :]

    lax.fori_loop(0, nC, bwd, jnp.zeros((1, LANES), F32))


def rg_lru(xg, conv_w, conv_b, wa, ba, wx, bx, lam, B, S, C=128):
    assert S % C == 0
    xv = xg.reshape(B, S, 2 * D_RNN)
    lc = (-LRU_C * jax.nn.softplus(-lam)).reshape(2, LRU_BLOCKS, 1, LRU_BW)
    vec = lambda t: t.reshape(2, LRU_BLOCKS, 1, LRU_BW)
    wspec = pl.BlockSpec((2, None, LRU_BW, LRU_BW), lambda b, n: (0, n, 0, 0))
    vspec = pl.BlockSpec((2, None, 1, LRU_BW), lambda b, n: (0, n, 0, 0))
    out = pl.pallas_call(
        functools.partial(_lru_kernel, S=S, C=C),
        grid=(B, LRU_BLOCKS),
        in_specs=[pl.BlockSpec((1, S, LRU_BW), lambda b, n: (b, 0, n)),
                  pl.BlockSpec((1, S, LRU_BW), lambda b, n: (b, 0, LRU_BLOCKS + n)),
                  pl.BlockSpec((CONV_W, LRU_BW), lambda b, n: (0, n)),
                  pl.BlockSpec((1, LRU_BW), lambda b, n: (0, n)),
                  wspec, vspec, wspec, vspec, vspec],
        out_specs=pl.BlockSpec((1, S, LRU_BW), lambda b, n: (b, 0, n)),
        out_shape=jax.ShapeDtypeStruct((B, S, D_RNN), BF16),
        scratch_shapes=[pltpu.VMEM((S, LRU_BW), F32)],
        compiler_params=_params(("parallel", "parallel")),
        name="rg_lru",
    )(xv, xv, conv_w, conv_b.reshape(1, D_RNN), wa.astype(BF16), vec(ba), wx.astype(BF16), vec(bx), lc)
    return out.reshape(B * S, D_RNN)


def _diff_bias(table, t, nd):
    d = jnp.arange(-nd, nd + 1, dtype=jnp.int32)[:, None, None] * t
    rel = d + jnp.arange(t, dtype=jnp.int32)[None, None, :] - jnp.arange(t, dtype=jnp.int32)[None, :, None]
    return table[_t5_bucket(rel)].astype(F32).transpose(3, 0, 1, 2)


def _diff_kernel(lam_ref, q_ref, k_ref, v_ref, b_ref, g_ref, o_ref, m_sc, l_sc, acc_sc, *, nk, mult):
    kv = pl.program_id(3)

    @pl.when(kv == 0)
    def _():
        m_sc[...] = jnp.full(m_sc.shape, -jnp.inf, F32)
        l_sc[...] = jnp.zeros(l_sc.shape, F32)
        acc_sc[...] = jnp.zeros(acc_sc.shape, F32)

    q, k, v = q_ref[0], k_ref[0], v_ref[0]
    bias = b_ref[0, 0]
    for mi in range(2):
        sl = slice(mi * DK_C, (mi + 1) * DK_C)
        s = lax.dot_general(q[:, sl], k[:, sl], (((1,), (1,)), ((), ())), preferred_element_type=F32) + bias
        m_prev = m_sc[mi]
        m_new = jnp.maximum(m_prev, jnp.max(s, axis=-1, keepdims=True))
        alpha = jnp.exp(m_prev - m_new)
        p = jnp.exp(s - m_new)
        l_sc[mi] = alpha * l_sc[mi] + jnp.sum(p, axis=-1, keepdims=True)
        acc_sc[mi] = alpha * acc_sc[mi] + jnp.dot(p.astype(BF16), v, preferred_element_type=F32)
        m_sc[mi] = m_new

    @pl.when(kv == nk - 1)
    def _():
        o = acc_sc[0] / l_sc[0] - lam_ref[0, 0] * (acc_sc[1] / l_sc[1])
        o_ref[0] = (_rms(o, g_ref[...]) * mult).astype(o_ref.dtype)


def diff_attention(z, table, lam, subln, lambda_init, B, S, t=512):
    t = min(t, S)
    assert S % t == 0
    n = S // t
    nd = MAX_DISTANCE // 2 // t + 2
    bias = _diff_bias(table, t, nd)
    zv = z.reshape(B, S, z.shape[1])
    blk = 2 * DK_C
    assert blk == DV_C
    kb, vb = H_C, 2 * H_C
    out = pl.pallas_call(
        functools.partial(_diff_kernel, nk=n, mult=1.0 - lambda_init),
        grid=(B, H_C, n, n),
        in_specs=[pl.BlockSpec(memory_space=pltpu.SMEM),
                  pl.BlockSpec((1, t, blk), lambda b, h, i, j: (b, i, h)),
                  pl.BlockSpec((1, t, blk), lambda b, h, i, j: (b, j, kb + h)),
                  pl.BlockSpec((1, t, DV_C), lambda b, h, i, j: (b, j, vb + h)),
                  pl.BlockSpec((1, 1, t, t), lambda b, h, i, j: (h, jnp.clip(j - i, -nd, nd) + nd, 0, 0)),
                  pl.BlockSpec((1, DV_C), lambda b, h, i, j: (0, 0))],
        out_specs=pl.BlockSpec((1, t, DV_C), lambda b, h, i, j: (b, i, h)),
        out_shape=jax.ShapeDtypeStruct((B, S, H_C * DV_C), BF16),
        scratch_shapes=[pltpu.VMEM((2, t, 1), F32), pltpu.VMEM((2, t, 1), F32), pltpu.VMEM((2, t, DV_C), F32)],
        compiler_params=_params(("parallel", "parallel", "parallel", "arbitrary")),
        name="diff_attn",
    )(lam.reshape(1, 1), zv, zv, zv, bias, subln.reshape(1, DV_C))
    return out.reshape(B * S, H_C * DV_C)


def _matmul_res_kernel(a_ref, w_ref, x_ref, o_ref):
    o_ref[...] = x_ref[...] + jnp.dot(a_ref[...], w_ref[...], preferred_element_type=F32)


def matmul_residual(a, w, x, tm):
    T, D = x.shape
    K = a.shape[1]
    tm = min(tm, T)
    assert T % tm == 0
    return pl.pallas_call(
        _matmul_res_kernel,
        grid=(T // tm,),
        in_specs=[pl.BlockSpec((tm, K), lambda i: (i, 0)),
                  pl.BlockSpec((K, D), lambda i: (0, 0)),
                  pl.BlockSpec((tm, D), lambda i: (i, 0))],
        out_specs=pl.BlockSpec((tm, D), lambda i: (i, 0)),
        out_shape=jax.ShapeDtypeStruct((T, D), F32),
        compiler_params=_params(("parallel",)),
        name="matmul_residual",
    )(a, w, x)


def _cross_kernel(x_ref, g_ref, wq_ref, kv_ref, wo_ref, o_ref):
    x = x_ref[0]
    h = _rms(x, g_ref[...]).astype(BF16)
    q = jnp.dot(h, wq_ref[...], preferred_element_type=F32).astype(BF16)
    kv = kv_ref[0]
    hw = H_X * DH_X
    outs = []
    for hd in range(H_X):
        sl = slice(hd * DH_X, (hd + 1) * DH_X)
        s = lax.dot_general(q[:, sl], kv[:, sl], (((1,), (1,)), ((), ())), preferred_element_type=F32)
        m = jnp.max(s, axis=-1, keepdims=True)
        p = jnp.exp(s - m)
        l = jnp.sum(p, axis=-1, keepdims=True)
        vh = kv[:, hw + hd * DH_X: hw + (hd + 1) * DH_X]
        outs.append((jnp.dot(p.astype(BF16), vh, preferred_element_type=F32) / l).astype(BF16))
    o = jnp.concatenate(outs, axis=-1)
    o_ref[0] = x + jnp.dot(o, wo_ref[...], preferred_element_type=F32)


def cross_attention(x, g, wq, kvn, wo, B, S, tm):
    D = x.shape[1]
    tm = min(tm, S)
    assert S % tm == 0
    hw = H_X * DH_X
    out = pl.pallas_call(
        _cross_kernel,
        grid=(B, S // tm),
        in_specs=[pl.BlockSpec((1, tm, D), lambda b, i: (b, i, 0)),
                  pl.BlockSpec((1, D), lambda b, i: (0, 0)),
                  pl.BlockSpec((D, hw), lambda b, i: (0, 0)),
                  pl.BlockSpec((1, N_MEM, 2 * hw), lambda b, i: (b, 0, 0)),
                  pl.BlockSpec((hw, D), lambda b, i: (0, 0))],
        out_specs=pl.BlockSpec((1, tm, D), lambda b, i: (b, i, 0)),
        out_shape=jax.ShapeDtypeStruct((B, S, D), F32),
        compiler_params=_params(("parallel", "parallel")),
        name="cross_attn",
    )(x.reshape(B, S, D), g.reshape(1, D), wq, kvn.reshape(B, N_MEM, 2 * hw), wo)
    return out.reshape(B * S, D)


ROUTE_BIG = 1 << 20


def _router_kernel(x_ref, g_ref, w_ref, b_ref, o_ref):
    h = _rms(x_ref[...], g_ref[...])
    lg = jnp.dot(h, w_ref[...], preferred_element_type=F32, precision=lax.Precision.HIGHEST) + b_ref[...]
    lane = lax.broadcasted_iota(jnp.int32, lg.shape, 1)
    first = lambda mask: jnp.min(jnp.where(mask, lane, ROUTE_BIG), axis=-1, keepdims=True)
    is_g = lane < N_GROUPS
    gl = jnp.where(is_g, lg, -jnp.inf)
    gmax = jnp.max(gl, axis=-1, keepdims=True)
    gidx = first(gl == gmax)
    pg = 1.0 / jnp.sum(jnp.where(is_g, jnp.exp(lg - gmax), 0.0), axis=-1, keepdims=True)
    e_id = lane - N_GROUPS
    in_grp = (e_id >= 0) & (e_id < N_EXPERTS) & ((e_id // EXPERTS_PER_GROUP) == gidx)
    el = jnp.where(in_grp, lg, -jnp.inf)
    v0 = jnp.max(el, axis=-1, keepdims=True)
    i0 = first(el == v0)
    el2 = jnp.where(lane == i0, -jnp.inf, el)
    v1 = jnp.max(el2, axis=-1, keepdims=True)
    i1 = first(el2 == v1)
    t = jnp.exp(v1 - v0)
    w0 = pg / (1.0 + t)
    w1 = w0 * t
    res = jnp.where(lane == 0, (i0 - N_GROUPS).astype(F32),
                    jnp.where(lane == 1, (i1 - N_GROUPS).astype(F32),
                              jnp.where(lane == 2, w0, jnp.where(lane == 3, w1, 0.0))))
    o_ref[...] = res


def moe_router(x, g, wg, bg, we, be, tm):
    T, D = x.shape
    tm = min(tm, T)
    wr = jnp.zeros((D, LANES), F32).at[:, :N_GROUPS].set(wg).at[:, N_GROUPS:N_GROUPS + N_EXPERTS].set(we)
    br = jnp.zeros((1, LANES), F32).at[0, :N_GROUPS].set(bg).at[0, N_GROUPS:N_GROUPS + N_EXPERTS].set(be)
    return pl.pallas_call(
        _router_kernel,
        grid=(T // tm,),
        in_specs=[pl.BlockSpec((tm, D), lambda i: (i, 0)),
                  pl.BlockSpec((1, D), lambda i: (0, 0)),
                  pl.BlockSpec((D, LANES), lambda i: (0, 0)),
                  pl.BlockSpec((1, LANES), lambda i: (0, 0))],
        out_specs=pl.BlockSpec((tm, LANES), lambda i: (i, 0)),
        out_shape=jax.ShapeDtypeStruct((T, LANES), F32),
        compiler_params=_params(("parallel",)),
        name="moe_router",
    )(x, g.reshape(1, D), wr, br)


def _expert_kernel(be_ref, nu_ref, x_ref, g_ref, wg_ref, wu_ref, wd_ref, o_ref):
    i = pl.program_id(0)

    @pl.when(i < nu_ref[0])
    def _():
        h = _rms(x_ref[...], g_ref[...]).astype(BF16)
        a = jnp.dot(h, wg_ref[0], preferred_element_type=F32)
        b = jnp.dot(h, wu_ref[0], preferred_element_type=F32)
        hid = (jax.nn.silu(a) * b).astype(BF16)
        o_ref[...] = jnp.dot(hid, wd_ref[0], preferred_element_type=F32).astype(o_ref.dtype)

    @pl.when(i >= nu_ref[0])
    def _():
        o_ref[...] = jnp.zeros(o_ref.shape, o_ref.dtype)


def moe_experts(xg, g, blk_e, n_used, w_gate, w_up, w_down, blk):
    cap, D = xg.shape
    n_blk = cap // blk
    grid_spec = pltpu.PrefetchScalarGridSpec(
        num_scalar_prefetch=2,
        grid=(n_blk,),
        in_specs=[pl.BlockSpec((blk, D), lambda i, be, nu: (i, 0)),
                  pl.BlockSpec((1, D), lambda i, be, nu: (0, 0)),
                  pl.BlockSpec((1, D, D_FF), lambda i, be, nu: (be[i], 0, 0)),
                  pl.BlockSpec((1, D, D_FF), lambda i, be, nu: (be[i], 0, 0)),
                  pl.BlockSpec((1, D_FF, D), lambda i, be, nu: (be[i], 0, 0))],
        out_specs=pl.BlockSpec((blk, D), lambda i, be, nu: (i, 0)),
    )
    return pl.pallas_call(
        _expert_kernel,
        grid_spec=grid_spec,
        out_shape=jax.ShapeDtypeStruct((cap, D), BF16),
        compiler_params=_params(("arbitrary",)),
        name="moe_experts",
    )(blk_e, n_used, xg, g.reshape(1, D), w_gate, w_up, w_down)


def _route_plan(routed, blk):
    T = routed.shape[0]
    e = routed[:, :2].astype(jnp.int32)
    flat_e = e.reshape(-1)
    n_rows = 2 * T
    onehot = (flat_e[:, None] == jnp.arange(N_EXPERTS, dtype=jnp.int32)[None, :]).astype(jnp.int32)
    cum = jnp.cumsum(onehot, axis=0)
    rank = jnp.sum(cum * onehot, axis=1) - 1
    counts = cum[-1]
    padded = (counts + blk - 1) // blk * blk
    pad_end = jnp.cumsum(padded)
    pad_start = pad_end - padded
    dest = pad_start[flat_e] + rank
    n_blk = -(-n_rows // blk) + N_EXPERTS
    cap = n_blk * blk
    flat_t = jnp.repeat(jnp.arange(T, dtype=jnp.int32), 2)
    buf_tok = jnp.zeros((cap,), jnp.int32).at[dest].set(flat_t)
    blk_e = jnp.minimum(jnp.searchsorted(pad_end, jnp.arange(n_blk, dtype=jnp.int32) * blk, side='right'),
                        N_EXPERTS - 1).astype(jnp.int32)
    n_used = (pad_end[-1:] // blk).astype(jnp.int32)
    return buf_tok, dest.reshape(T, 2), blk_e, n_used


def hier_moe(x, g, wg, bg, we, be, w_gate, w_up, w_down, blk=512, tm=512):
    routed = moe_router(x, g, wg, bg, we, be, tm)
    buf_tok, dest, blk_e, n_used = _route_plan(routed, blk)
    out = moe_experts(x[buf_tok], g, blk_e, n_used, w_gate, w_up, w_down, blk)
    w = routed[:, 2:4]
    y = out[dest[:, 0]].astype(F32) * w[:, 0:1] + out[dest[:, 1]].astype(F32) * w[:, 1:2]
    return x + y


def _norm_kernel(x_ref, g_ref, o_ref):
    o_ref[...] = _rms(x_ref[...], g_ref[...])


def final_norm(x, g, tm):
    T, D = x.shape
    tm = min(tm, T)
    return pl.pallas_call(
        _norm_kernel,
        grid=(T // tm,),
        in_specs=[pl.BlockSpec((tm, D), lambda i: (i, 0)), pl.BlockSpec((1, D), lambda i: (0, 0))],
        out_specs=pl.BlockSpec((tm, D), lambda i: (i, 0)),
        out_shape=jax.ShapeDtypeStruct((T, D), F32),
        compiler_params=_params(("parallel",)),
        name="final_norm",
    )(x, g.reshape(1, D))


def _lambda_init(layer):
    return 0.8 - 0.6 * math.exp(-0.3 * layer)


def _prep_weights(p):
    w = dict(p)
    ab = p["ab_w_in"]
    w["ab_w_qkv"] = jnp.concatenate([ab[:, :, :W_A] * DH_A ** -0.5, ab[:, :, W_A:3 * W_A]], axis=-1).astype(BF16)
    w["ab_w_xg"] = ab[:, :, 3 * W_A:].astype(BF16)
    w["ab_w_out"] = p["ab_w_out"].astype(BF16)
    qk = H_C * 2 * DK_C
    cw = p["c_w_in"]
    w["c_w_in"] = jnp.concatenate([cw[:, :, :qk] * DK_C ** -0.5, cw[:, :, qk:]], axis=-1).astype(BF16)
    w["c_w_out"] = p["c_w_out"].astype(BF16)
    w["x_wq"] = (p["x_wq"] * DH_X ** -0.5).astype(BF16)
    w["x_wkv"] = p["x_wkv"].astype(BF16)
    w["x_wo"] = p["x_wo"].astype(BF16)
    for n in ("moe_w_gate", "moe_w_up", "moe_w_down"):
        w[n] = p[n].astype(BF16)
    lv = p["c_lam"].astype(F32)
    w["c_lam_scalar"] = [jnp.exp(jnp.sum(lv[i, 0] * lv[i, 1])) - jnp.exp(jnp.sum(lv[i, 2] * lv[i, 3]))
                         + _lambda_init(2 * i + 1) for i in range(lv.shape[0])]
    return w


def _trunk(x3, mem3, w):
    B, S, D = x3.shape
    x = x3.reshape(B * S, D)
    mem = mem3.reshape(B * N_MEM, D)
    table = w["rel_bias"]
    for l in range(DEPTH):
        i = l // 2
        if l % 2 == 0:
            zqkv = norm_matmul(x, w["norm_mix"][l], w["ab_w_qkv"][i], BF16, 1024, 1024)
            xg = norm_matmul(x, w["norm_mix"][l], w["ab_w_xg"][i], F32, 1024, 1024)
            res = [dilated_attention(zqkv, table[:, :H_A], B, S, dil) for _, dil in DILATED_PATTERNS]
            attn = merge_patterns([r[0] for r in res], [r[1] for r in res], 1024)
            rec = rg_lru(xg, w["ab_conv_w"][i], w["ab_conv_b"][i], w["lru_wa"][i], w["lru_ba"][i],
                         w["lru_wx"][i], w["lru_bx"][i], w["lru_lam"][i], B, S)
            x = mix_out(attn, rec, w["ab_w_out"][i], x, 512)
        else:
            z = norm_matmul(x, w["norm_mix"][l], w["c_w_in"][i], BF16, 1024, 1024)
            o = diff_attention(z, table[:, H_A:], w["c_lam_scalar"][i], w["c_subln"][i], _lambda_init(l), B, S)
            x = matmul_residual(o, w["c_w_out"][i], x, 512)
        kvn = norm_matmul(mem, w["norm_mem"][l], w["x_wkv"][l], BF16, 1024, 1024)
        x = cross_attention(x, w["norm_cross"][l], w["x_wq"][l], kvn, w["x_wo"][l], B, S, 512)
        x = hier_moe(x, w["norm_ffn"][l], w["moe_wg"][l], w["moe_bg"][l], w["moe_we"][l], w["moe_be"][l],
                     w["moe_w_gate"][l], w["moe_w_up"][l], w["moe_w_down"][l])
    return final_norm(x, w["norm_final"], 1024).reshape(B, S, D)


def kernel(x_prompt, x_sample, mem_prompt, mem_sample, rel_bias, ab_w_in, ab_conv_w, ab_conv_b, lru_wa, lru_ba, lru_wx, lru_bx, lru_lam, ab_w_out, c_w_in, c_lam, c_subln, c_w_out, norm_mix, norm_cross, norm_mem, x_wq, x_wkv, x_wo, norm_ffn, moe_wg, moe_bg, moe_we, moe_be, moe_w_gate, moe_w_up, moe_w_down, norm_final):
    w = _prep_weights(dict(
        rel_bias=rel_bias, ab_w_in=ab_w_in, ab_conv_w=ab_conv_w, ab_conv_b=ab_conv_b, lru_wa=lru_wa, lru_ba=lru_ba,
        lru_wx=lru_wx, lru_bx=lru_bx, lru_lam=lru_lam, ab_w_out=ab_w_out, c_w_in=c_w_in, c_lam=c_lam,
        c_subln=c_subln, c_w_out=c_w_out, norm_mix=norm_mix, norm_cross=norm_cross, norm_mem=norm_mem, x_wq=x_wq,
        x_wkv=x_wkv, x_wo=x_wo, norm_ffn=norm_ffn, moe_wg=moe_wg, moe_bg=moe_bg, moe_we=moe_we, moe_be=moe_be,
        moe_w_gate=moe_w_gate, moe_w_up=moe_w_up, moe_w_down=moe_w_down, norm_final=norm_final))
    return _trunk(x_prompt, mem_prompt, w), _trunk(x_sample, mem_sample, w)
```

```python
import functools
import math

import jax
import jax.numpy as jnp
from jax import lax
from jax.experimental import pallas as pl
from jax.experimental.pallas import tpu as pltpu

F32 = jnp.float32
BF16 = jnp.bfloat16

D_MODEL = 2048
DEPTH = 4
H_A, DH_A = 16, 64
W_A = H_A * DH_A
DILATED_PATTERNS = ((128, 1), (512, 4), (2048, 16))
D_RNN = 1024
LRU_BLOCKS, LRU_BW = 8, 128
CONV_W = 4
LRU_C = 8.0
H_C, DK_C, DV_C = 8, 128, 256
NUM_BUCKETS, MAX_DISTANCE = 32, 2048
H_X, DH_X, N_MEM = 4, 128, 256
N_GROUPS, EXPERTS_PER_GROUP = 4, 4
N_EXPERTS = N_GROUPS * EXPERTS_PER_GROUP
D_FF = 1024
EPS = 1e-6
NEG = -1e30
LOG2E = math.log2(math.e)

LANES = 128
VMEM_LIMIT = 56 * 1024 * 1024


def _params(sem):
    return pltpu.CompilerParams(dimension_semantics=sem, vmem_limit_bytes=VMEM_LIMIT)


def _rms(x, g):
    return x * lax.rsqrt(jnp.mean(x * x, axis=-1, keepdims=True) + EPS) * g


def _norm_matmul_kernel(x_ref, g_ref, w_ref, o_ref, h_ref):
    @pl.when(pl.program_id(1) == 0)
    def _():
        h_ref[...] = _rms(x_ref[...], g_ref[...]).astype(BF16)

    o_ref[...] = jnp.dot(h_ref[...], w_ref[...], preferred_element_type=F32).astype(o_ref.dtype)


def norm_matmul(x, g, w, out_dtype, tm, tn):
    T, D = x.shape
    N = w.shape[1]
    tm, tn = min(tm, T), min(tn, N)
    assert T % tm == 0 and N % tn == 0
    return pl.pallas_call(
        _norm_matmul_kernel,
        grid=(T // tm, N // tn),
        in_specs=[pl.BlockSpec((tm, D), lambda i, j: (i, 0)),
                  pl.BlockSpec((1, D), lambda i, j: (0, 0)),
                  pl.BlockSpec((D, tn), lambda i, j: (0, j))],
        out_specs=pl.BlockSpec((tm, tn), lambda i, j: (i, j)),
        out_shape=jax.ShapeDtypeStruct((T, N), out_dtype),
        scratch_shapes=[pltpu.VMEM((tm, D), BF16)],
        compiler_params=_params(("parallel", "arbitrary")),
        name="norm_matmul",
    )(x, g.reshape(1, D), w)


def _mix_out_kernel(a1_ref, a2_ref, w_ref, x_ref, o_ref):
    k1 = a1_ref.shape[1]
    acc = jnp.dot(a1_ref[...], w_ref[0:k1, :], preferred_element_type=F32)
    acc += jnp.dot(a2_ref[...], w_ref[k1:, :], preferred_element_type=F32)
    o_ref[...] = x_ref[...] + acc


def mix_out(a1, a2, w, x, tm):
    T, D = x.shape
    K1, K2 = a1.shape[1], a2.shape[1]
    tm = min(tm, T)
    assert T % tm == 0
    return pl.pallas_call(
        _mix_out_kernel,
        grid=(T // tm,),
        in_specs=[pl.BlockSpec((tm, K1), lambda i: (i, 0)),
                  pl.BlockSpec((tm, K2), lambda i: (i, 0)),
                  pl.BlockSpec((K1 + K2, D), lambda i: (0, 0)),
                  pl.BlockSpec((tm, D), lambda i: (i, 0))],
        out_specs=pl.BlockSpec((tm, D), lambda i: (i, 0)),
        out_shape=jax.ShapeDtypeStruct((T, D), F32),
        compiler_params=_params(("parallel",)),
        name="mix_out",
    )(a1, a2, w, x)


def _t5_bucket(rel):
    half = NUM_BUCKETS // 2
    exact = half // 2
    n = jnp.abs(rel)
    large = exact + (jnp.log(jnp.maximum(n, 1).astype(F32) / exact)
                     / math.log(MAX_DISTANCE / exact) * (half - exact)).astype(jnp.int32)
    large = jnp.minimum(large, half - 1)
    return (rel > 0).astype(jnp.int32) * half + jnp.where(n < exact, n, large)


DIL_RADIUS = 64


def _dilated_bias(table, dil, tq):
    win = tq + 2 * DIL_RADIUS
    r = jnp.arange(tq, dtype=jnp.int32)[:, None]
    c = jnp.arange(win, dtype=jnp.int32)[None, :]
    tiles = []
    for off in (0, -DIL_RADIUS, -2 * DIL_RADIUS):
        rel = off + c - r
        b = table[_t5_bucket(rel * dil)].astype(F32).transpose(2, 0, 1)
        tiles.append(jnp.where((jnp.abs(rel) <= DIL_RADIUS)[None], b, NEG))
    return jnp.stack(tiles)


def _dilated_kernel(q_ref, k_ref, v_ref, b_ref, o_ref, lse_ref, *, L, tq, unroll):
    nq = L // tq
    win = tq + 2 * DIL_RADIUS
    head0 = lax.broadcasted_iota(jnp.int32, (1, LANES), 1) < DH_A

    def body(i, carry):
        i0 = pl.multiple_of(i * tq, tq)
        start = pl.multiple_of(jnp.clip(i0 - DIL_RADIUS, 0, L - win), DIL_RADIUS)
        var = jnp.where(i == 0, 0, jnp.where(i == nq - 1, 2, 1))
        q = q_ref[0, pl.ds(i0, tq), :]
        k = k_ref[0, pl.ds(start, win), :]
        v = v_ref[0, pl.ds(start, win), :]
        outs, lses = [], []
        for h in range(2):
            qh = jnp.where(head0 if h == 0 else jnp.logical_not(head0), q, jnp.zeros_like(q))
            s = lax.dot_general(qh, k, (((1,), (1,)), ((), ())), preferred_element_type=F32)
            s = s + b_ref[var, h]
            m = jnp.max(s, axis=-1, keepdims=True)
            p = jnp.exp(s - m)
            l = jnp.sum(p, axis=-1, keepdims=True)
            outs.append(jnp.dot(p.astype(BF16), v, preferred_element_type=F32) / l)
            lses.append(m + jnp.log(l))
        o_ref[0, pl.ds(i0, tq), :] = jnp.where(head0, outs[0], outs[1]).astype(o_ref.dtype)
        lse_ref[0, pl.ds(i0, tq), :] = jnp.where(head0, lses[0], lses[1])
        return carry

    lax.fori_loop(0, nq, body, 0, unroll=min(unroll, nq))


def dilated_attention(zqkv, table, B, S, dil, tq=128, unroll=8):
    L = S // dil
    assert S % dil == 0 and L % tq == 0 and L >= tq + 2 * DIL_RADIUS
    zv = zqkv.reshape(B, L, dil * 3 * W_A)
    nb = 3 * W_A // LANES
    hb = W_A // LANES
    bias = _dilated_bias(table, dil, tq)
    win = tq + 2 * DIL_RADIUS
    o, lse = pl.pallas_call(
        functools.partial(_dilated_kernel, L=L, tq=tq, unroll=unroll),
        grid=(B, dil, hb),
        in_specs=[pl.BlockSpec((1, L, LANES), lambda b, r, h: (b, 0, r * nb + h)),
                  pl.BlockSpec((1, L, LANES), lambda b, r, h: (b, 0, r * nb + hb + h)),
                  pl.BlockSpec((1, L, LANES), lambda b, r, h: (b, 0, r * nb + 2 * hb + h)),
                  pl.BlockSpec((3, 2, tq, win), lambda b, r, h: (0, h, 0, 0))],
        out_specs=[pl.BlockSpec((1, L, LANES), lambda b, r, h: (b, 0, r * hb + h)),
                   pl.BlockSpec((1, L, LANES), lambda b, r, h: (b, 0, r * hb + h))],
        out_shape=[jax.ShapeDtypeStruct((B, L, dil * W_A), BF16),
                   jax.ShapeDtypeStruct((B, L, dil * W_A), F32)],
        compiler_params=_params(("parallel", "parallel", "parallel")),
        name=f"dilated_attn_d{dil}",
    )(zv, zv, zv, bias)
    return o.reshape(B * S, W_A), lse.reshape(B * S, W_A)


def _merge_kernel(o1, o2, o3, l1, l2, l3, out_ref):
    a, b, c = l1[...], l2[...], l3[...]
    m = jnp.maximum(jnp.maximum(a, b), c)
    ea, eb, ec = jnp.exp(a - m), jnp.exp(b - m), jnp.exp(c - m)
    num = ea * o1[...].astype(F32) + eb * o2[...].astype(F32) + ec * o3[...].astype(F32)
    out_ref[...] = (num / (ea + eb + ec)).astype(out_ref.dtype)


def merge_patterns(os_, lses, tm):
    T, W = os_[0].shape
    tm = min(tm, T)
    spec = pl.BlockSpec((tm, W), lambda i: (i, 0))
    return pl.pallas_call(
        _merge_kernel,
        grid=(T // tm,),
        in_specs=[spec] * 6,
        out_specs=spec,
        out_shape=jax.ShapeDtypeStruct((T, W), BF16),
        compiler_params=_params(("parallel",)),
        name="merge_patterns",
    )(*os_, *lses)


def _lru_kernel(xr_ref, gr_ref, cw_ref, cb_ref, wa_ref, ba_ref, wx_ref, bx_ref, lc_ref, o_ref, hf_ref, *, S, C):
    nC = S // C
    row = lax.broadcasted_iota(jnp.int32, (C, LANES), 0)
    E = C + 16

    def conv_chunk(c):
        t0 = pl.multiple_of(c * C, C)
        cur = xr_ref[0, pl.ds(t0, C), :]
        pstart = pl.multiple_of(jnp.maximum(t0 - 8, 0), 8)
        nstart = pl.multiple_of(jnp.minimum(t0 + C, S - 8), 8)
        prev8 = jnp.where(c > 0, xr_ref[0, pl.ds(pstart, 8), :], 0.0)
        next8 = jnp.where(c < nC - 1, xr_ref[0, pl.ds(nstart, 8), :], 0.0)
        ext = jnp.concatenate([prev8, cur, next8], axis=0)
        xc = cb_ref[...] + ext[8:8 + C] * cw_ref[1:2, :]
        xc += pltpu.roll(ext, 1, 0)[8:8 + C] * cw_ref[0:1, :]
        xc += pltpu.roll(ext, E - 1, 0)[8:8 + C] * cw_ref[2:3, :]
        xc += pltpu.roll(ext, E - 2, 0)[8:8 + C] * cw_ref[3:4, :]
        return t0, xc

    def gates(xc, d):
        xb = xc.astype(BF16)
        r = jax.nn.sigmoid(jnp.dot(xb, wa_ref[d], preferred_element_type=F32) + ba_ref[d])
        i = jax.nn.sigmoid(jnp.dot(xb, wx_ref[d], preferred_element_type=F32) + bx_ref[d])
        log_a = lc_ref[d] * r
        a = jnp.exp(log_a)
        u = jnp.sqrt(-jnp.tanh(log_a) * (a * a + 1.0)) * (i * xc)
        return a, u

    def scan(a, u, reverse):
        k = 1
        while k < C:
            shift = C - k if reverse else k
            valid = (row < C - k) if reverse else (row >= k)
            u = u + a * jnp.where(valid, pltpu.roll(u, shift, 0), 0.0)
            a = a * jnp.where(valid, pltpu.roll(a, shift, 0), 1.0)
            k *= 2
        return a, u

    def fwd(c, carry):
        t0, xc = conv_chunk(c)
        a, u = scan(*gates(xc, 0), False)
        h = u + a * carry
        hf_ref[pl.ds(t0, C), :] = h
        return h[C - 1:C, :]

    lax.fori_loop(0, nC, fwd, jnp.zeros((1, LANES), F32))

    def bwd(j, carry):
        t0, xc = conv_chunk(nC - 1 - j)
        a, u = scan(*gates(xc, 1), True)
        h = u + a * carry
        g = jax.nn.gelu(gr_ref[0, pl.ds(t0, C), :])
        o_ref[0, pl.ds(t0, C), :] = (g * (hf_ref[pl.ds(t0, C), :] + h)).astype(o_ref.dtype)
        return h[0:1, :]

    lax.fori_loop(0, nC, bwd, jnp.zeros((1, LANES), F32))


def rg_lru(xg, conv_w, conv_b, wa, ba, wx, bx, lam, B, S, C=128):
    assert S % C == 0
    xv = xg.reshape(B, S, 2 * D_RNN)
    lc = (-LRU_C * jax.nn.softplus(-lam)).reshape(2, LRU_BLOCKS, 1, LRU_BW)
    vec = lambda t: t.reshape(2, LRU_BLOCKS, 1, LRU_BW)
    wspec = pl.BlockSpec((2, None, LRU_BW, LRU_BW), lambda b, n: (0, n, 0, 0))
    vspec = pl.BlockSpec((2, None, 1, LRU_BW), lambda b, n: (0, n, 0, 0))
    out = pl.pallas_call(
        functools.partial(_lru_kernel, S=S, C=C),
        grid=(B, LRU_BLOCKS),
        in_specs=[pl.BlockSpec((1, S, LRU_BW), lambda b, n: (b, 0, n)),
                  pl.BlockSpec((1, S, LRU_BW), lambda b, n: (b, 0, LRU_BLOCKS + n)),
                  pl.BlockSpec((CONV_W, LRU_BW), lambda b, n: (0, n)),
                  pl.BlockSpec((1, LRU_BW), lambda b, n: (0, n)),
                  wspec, vspec, wspec, vspec, vspec],
        out_specs=pl.BlockSpec((1, S, LRU_BW), lambda b, n: (b, 0, n)),
        out_shape=jax.ShapeDtypeStruct((B, S, D_RNN), BF16),
        scratch_shapes=[pltpu.VMEM((S, LRU_BW), F32)],
        compiler_params=_params(("parallel", "parallel")),
        name="rg_lru",
    )(xv, xv, conv_w, conv_b.reshape(1, D_RNN), wa.astype(BF16), vec(ba), wx.astype(BF16), vec(bx), lc)
    return out.reshape(B * S, D_RNN)


def _diff_bias(table, t, nd):
    d = jnp.arange(-nd, nd + 1, dtype=jnp.int32)[:, None, None] * t
    rel = d + jnp.arange(t, dtype=jnp.int32)[None, None, :] - jnp.arange(t, dtype=jnp.int32)[None, :, None]
    return table[_t5_bucket(rel)].astype(F32).transpose(3, 0, 1, 2)


def _diff_kernel(lam_ref, q_ref, k_ref, v_ref, b_ref, g_ref, o_ref, m_sc, l_sc, acc_sc, *, nk, mult, rs, unroll):
    kv = pl.program_id(3)
    t = q_ref.shape[1]
    tk = k_ref.shape[1]

    @pl.when(kv == 0)
    def _():
        m_sc[...] = jnp.full(m_sc.shape, -jnp.inf, F32)
        l_sc[...] = jnp.zeros(l_sc.shape, F32)
        acc_sc[...] = jnp.zeros(acc_sc.shape, F32)

    def sub(r, carry):
        r0 = pl.multiple_of(r * rs, rs)
        rows = pl.ds(r0, rs)
        q = q_ref[0, rows, :]
        bias = b_ref[0, 0, rows, :]
        for mi in range(2):
            sl = slice(mi * DK_C, (mi + 1) * DK_C)
            s = lax.dot_general(q[:, sl], k_ref[0, :, sl], (((1,), (1,)), ((), ())),
                                preferred_element_type=F32) + bias
            m_prev = m_sc[mi, rows, :]
            m_new = jnp.maximum(m_prev, jnp.max(s, axis=-1, keepdims=True))
            alpha = jnp.exp2(m_prev - m_new)
            p = jnp.exp2(s - pltpu.repeat(m_new, tk // LANES, axis=1))
            l_sc[mi, rows, :] = alpha * l_sc[mi, rows, :] + jnp.sum(p, axis=-1, keepdims=True)
            acc_sc[mi, rows, :] = (pltpu.repeat(alpha, DV_C // LANES, axis=1) * acc_sc[mi, rows, :]
                                   + jnp.dot(p.astype(BF16), v_ref[0], preferred_element_type=F32))
            m_sc[mi, rows, :] = m_new
        return carry

    lax.fori_loop(0, t // rs, sub, 0, unroll=min(unroll, t // rs))

    @pl.when(kv == nk - 1)
    def _():
        rep = lambda x: pltpu.repeat(x, DV_C // LANES, axis=1)
        o = acc_sc[0] / rep(l_sc[0]) - lam_ref[0, 0] * (acc_sc[1] / rep(l_sc[1]))
        o_ref[0] = (_rms(o, g_ref[...]) * mult).astype(o_ref.dtype)


def diff_attention(z, table, lam, subln, lambda_init, B, S, t=512, rs=128, unroll=4):
    t = min(t, S)
    assert S % t == 0 and t % rs == 0
    n = S // t
    nd = MAX_DISTANCE // 2 // t + 2
    bias = _diff_bias(table, t, nd) * LOG2E
    zv = z.reshape(B, S, z.shape[1])
    blk = 2 * DK_C
    assert blk == DV_C
    kb, vb = H_C, 2 * H_C
    out = pl.pallas_call(
        functools.partial(_diff_kernel, nk=n, mult=1.0 - lambda_init, rs=rs, unroll=unroll),
        grid=(B, H_C, n, n),
        in_specs=[pl.BlockSpec(memory_space=pltpu.SMEM),
                  pl.BlockSpec((1, t, blk), lambda b, h, i, j: (b, i, h)),
                  pl.BlockSpec((1, t, blk), lambda b, h, i, j: (b, j, kb + h)),
                  pl.BlockSpec((1, t, DV_C), lambda b, h, i, j: (b, j, vb + h)),
                  pl.BlockSpec((1, 1, t, t), lambda b, h, i, j: (h, jnp.clip(j - i, -nd, nd) + nd, 0, 0)),
                  pl.BlockSpec((1, DV_C), lambda b, h, i, j: (0, 0))],
        out_specs=pl.BlockSpec((1, t, DV_C), lambda b, h, i, j: (b, i, h)),
        out_shape=jax.ShapeDtypeStruct((B, S, H_C * DV_C), BF16),
        scratch_shapes=[pltpu.VMEM((2, t, LANES), F32), pltpu.VMEM((2, t, LANES), F32),
                        pltpu.VMEM((2, t, DV_C), F32)],
        compiler_params=_params(("parallel", "parallel", "parallel", "arbitrary")),
        name="diff_attn",
    )(lam.reshape(1, 1), zv, zv, zv, bias, subln.reshape(1, DV_C))
    return out.reshape(B * S, H_C * DV_C)


def _matmul_res_kernel(a_ref, w_ref, x_ref, o_ref):
    o_ref[...] = x_ref[...] + jnp.dot(a_ref[...], w_ref[...], preferred_element_type=F32)


def matmul_residual(a, w, x, tm):
    T, D = x.shape
    K = a.shape[1]
    tm = min(tm, T)
    assert T % tm == 0
    return pl.pallas_call(
        _matmul_res_kernel,
        grid=(T // tm,),
        in_specs=[pl.BlockSpec((tm, K), lambda i: (i, 0)),
                  pl.BlockSpec((K, D), lambda i: (0, 0)),
                  pl.BlockSpec((tm, D), lambda i: (i, 0))],
        out_specs=pl.BlockSpec((tm, D), lambda i: (i, 0)),
        out_shape=jax.ShapeDtypeStruct((T, D), F32),
        compiler_params=_params(("parallel",)),
        name="matmul_residual",
    )(a, w, x)


def _cross_kernel(x_ref, g_ref, wq_ref, kv_ref, wo_ref, o_ref):
    x = x_ref[0]
    h = _rms(x, g_ref[...]).astype(BF16)
    q = jnp.dot(h, wq_ref[...], preferred_element_type=F32).astype(BF16)
    kv = kv_ref[0]
    hw = H_X * DH_X
    outs = []
    for hd in range(H_X):
        sl = slice(hd * DH_X, (hd + 1) * DH_X)
        s = lax.dot_general(q[:, sl], kv[:, sl], (((1,), (1,)), ((), ())), preferred_element_type=F32)
        m = jnp.max(s, axis=-1, keepdims=True)
        p = jnp.exp(s - m)
        l = jnp.sum(p, axis=-1, keepdims=True)
        vh = kv[:, hw + hd * DH_X: hw + (hd + 1) * DH_X]
        outs.append((jnp.dot(p.astype(BF16), vh, preferred_element_type=F32) / l).astype(BF16))
    o = jnp.concatenate(outs, axis=-1)
    o_ref[0] = x + jnp.dot(o, wo_ref[...], preferred_element_type=F32)


def cross_attention(x, g, wq, kvn, wo, B, S, tm):
    D = x.shape[1]
    tm = min(tm, S)
    assert S % tm == 0
    hw = H_X * DH_X
    out = pl.pallas_call(
        _cross_kernel,
        grid=(B, S // tm),
        in_specs=[pl.BlockSpec((1, tm, D), lambda b, i: (b, i, 0)),
                  pl.BlockSpec((1, D), lambda b, i: (0, 0)),
                  pl.BlockSpec((D, hw), lambda b, i: (0, 0)),
                  pl.BlockSpec((1, N_MEM, 2 * hw), lambda b, i: (b, 0, 0)),
                  pl.BlockSpec((hw, D), lambda b, i: (0, 0))],
        out_specs=pl.BlockSpec((1, tm, D), lambda b, i: (b, i, 0)),
        out_shape=jax.ShapeDtypeStruct((B, S, D), F32),
        compiler_params=_params(("parallel", "parallel")),
        name="cross_attn",
    )(x.reshape(B, S, D), g.reshape(1, D), wq, kvn.reshape(B, N_MEM, 2 * hw), wo)
    return out.reshape(B * S, D)


ROUTE_BIG = 1 << 20


def _router_kernel(x_ref, g_ref, w_ref, b_ref, o_ref):
    h = _rms(x_ref[...], g_ref[...])
    lg = jnp.dot(h, w_ref[...], preferred_element_type=F32, precision=lax.Precision.HIGHEST) + b_ref[...]
    lane = lax.broadcasted_iota(jnp.int32, lg.shape, 1)
    first = lambda mask: jnp.min(jnp.where(mask, lane, ROUTE_BIG), axis=-1, keepdims=True)
    is_g = lane < N_GROUPS
    gl = jnp.where(is_g, lg, -jnp.inf)
    gmax = jnp.max(gl, axis=-1, keepdims=True)
    gidx = first(gl == gmax)
    pg = 1.0 / jnp.sum(jnp.where(is_g, jnp.exp(lg - gmax), 0.0), axis=-1, keepdims=True)
    e_id = lane - N_GROUPS
    in_grp = (e_id >= 0) & (e_id < N_EXPERTS) & ((e_id // EXPERTS_PER_GROUP) == gidx)
    el = jnp.where(in_grp, lg, -jnp.inf)
    v0 = jnp.max(el, axis=-1, keepdims=True)
    i0 = first(el == v0)
    el2 = jnp.where(lane == i0, -jnp.inf, el)
    v1 = jnp.max(el2, axis=-1, keepdims=True)
    i1 = first(el2 == v1)
    t = jnp.exp(v1 - v0)
    w0 = pg / (1.0 + t)
    w1 = w0 * t
    res = jnp.where(lane == 0, (i0 - N_GROUPS).astype(F32),
                    jnp.where(lane == 1, (i1 - N_GROUPS).astype(F32),
                              jnp.where(lane == 2, w0, jnp.where(lane == 3, w1, 0.0))))
    o_ref[...] = res


def moe_router(x, g, wg, bg, we, be, tm):
    T, D = x.shape
    tm = min(tm, T)
    wr = jnp.zeros((D, LANES), F32).at[:, :N_GROUPS].set(wg).at[:, N_GROUPS:N_GROUPS + N_EXPERTS].set(we)
    br = jnp.zeros((1, LANES), F32).at[0, :N_GROUPS].set(bg).at[0, N_GROUPS:N_GROUPS + N_EXPERTS].set(be)
    return pl.pallas_call(
        _router_kernel,
        grid=(T // tm,),
        in_specs=[pl.BlockSpec((tm, D), lambda i: (i, 0)),
                  pl.BlockSpec((1, D), lambda i: (0, 0)),
                  pl.BlockSpec((D, LANES), lambda i: (0, 0)),
                  pl.BlockSpec((1, LANES), lambda i: (0, 0))],
        out_specs=pl.BlockSpec((tm, LANES), lambda i: (i, 0)),
        out_shape=jax.ShapeDtypeStruct((T, LANES), F32),
        compiler_params=_params(("parallel",)),
        name="moe_router",
    )(x, g.reshape(1, D), wr, br)


def _expert_kernel(be_ref, nu_ref, x_ref, g_ref, wg_ref, wu_ref, wd_ref, o_ref):
    i = pl.program_id(0)

    @pl.when(i < nu_ref[0])
    def _():
        h = _rms(x_ref[...], g_ref[...]).astype(BF16)
        a = jnp.dot(h, wg_ref[0], preferred_element_type=F32)
        b = jnp.dot(h, wu_ref[0], preferred_element_type=F32)
        hid = (jax.nn.silu(a) * b).astype(BF16)
        o_ref[...] = jnp.dot(hid, wd_ref[0], preferred_element_type=F32).astype(o_ref.dtype)

    @pl.when(i >= nu_ref[0])
    def _():
        o_ref[...] = jnp.zeros(o_ref.shape, o_ref.dtype)


def moe_experts(xg, g, blk_e, n_used, w_gate, w_up, w_down, blk):
    cap, D = xg.shape
    n_blk = cap // blk
    grid_spec = pltpu.PrefetchScalarGridSpec(
        num_scalar_prefetch=2,
        grid=(n_blk,),
        in_specs=[pl.BlockSpec((blk, D), lambda i, be, nu: (i, 0)),
                  pl.BlockSpec((1, D), lambda i, be, nu: (0, 0)),
                  pl.BlockSpec((1, D, D_FF), lambda i, be, nu: (be[i], 0, 0)),
                  pl.BlockSpec((1, D, D_FF), lambda i, be, nu: (be[i], 0, 0)),
                  pl.BlockSpec((1, D_FF, D), lambda i, be, nu: (be[i], 0, 0))],
        out_specs=pl.BlockSpec((blk, D), lambda i, be, nu: (i, 0)),
    )
    return pl.pallas_call(
        _expert_kernel,
        grid_spec=grid_spec,
        out_shape=jax.ShapeDtypeStruct((cap, D), BF16),
        compiler_params=_params(("arbitrary",)),
        name="moe_experts",
    )(blk_e, n_used, xg, g.reshape(1, D), w_gate, w_up, w_down)


def _route_plan(routed, blk):
    T = routed.shape[0]
    e = routed[:, :2].astype(jnp.int32)
    flat_e = e.reshape(-1)
    n_rows = 2 * T
    onehot = (flat_e[:, None] == jnp.arange(N_EXPERTS, dtype=jnp.int32)[None, :]).astype(jnp.int32)
    cum = jnp.cumsum(onehot, axis=0)
    rank = jnp.sum(cum * onehot, axis=1) - 1
    counts = cum[-1]
    padded = (counts + blk - 1) // blk * blk
    pad_end = jnp.cumsum(padded)
    pad_start = pad_end - padded
    dest = pad_start[flat_e] + rank
    n_blk = -(-n_rows // blk) + N_EXPERTS
    cap = n_blk * blk
    flat_t = jnp.repeat(jnp.arange(T, dtype=jnp.int32), 2)
    buf_tok = jnp.zeros((cap,), jnp.int32).at[dest].set(flat_t)
    blk_e = jnp.minimum(jnp.searchsorted(pad_end, jnp.arange(n_blk, dtype=jnp.int32) * blk, side='right'),
                        N_EXPERTS - 1).astype(jnp.int32)
    n_used = (pad_end[-1:] // blk).astype(jnp.int32)
    return buf_tok, dest.reshape(T, 2), blk_e, n_used


def hier_moe(x, g, wg, bg, we, be, w_gate, w_up, w_down, blk=512, tm=512):
    routed = moe_router(x, g, wg, bg, we, be, tm)
    buf_tok, dest, blk_e, n_used = _route_plan(routed, blk)
    out = moe_experts(x[buf_tok], g, blk_e, n_used, w_gate, w_up, w_down, blk)
    w = routed[:, 2:4]
    y = out[dest[:, 0]].astype(F32) * w[:, 0:1] + out[dest[:, 1]].astype(F32) * w[:, 1:2]
    return x + y


def _norm_kernel(x_ref, g_ref, o_ref):
    o_ref[...] = _rms(x_ref[...], g_ref[...])


def final_norm(x, g, tm):
    T, D = x.shape
    tm = min(tm, T)
    return pl.pallas_call(
        _norm_kernel,
        grid=(T // tm,),
        in_specs=[pl.BlockSpec((tm, D), lambda i: (i, 0)), pl.BlockSpec((1, D), lambda i: (0, 0))],
        out_specs=pl.BlockSpec((tm, D), lambda i: (i, 0)),
        out_shape=jax.ShapeDtypeStruct((T, D), F32),
        compiler_params=_params(("parallel",)),
        name="final_norm",
    )(x, g.reshape(1, D))


def _lambda_init(layer):
    return 0.8 - 0.6 * math.exp(-0.3 * layer)


def _prep_weights(p):
    w = dict(p)
    ab = p["ab_w_in"]
    w["ab_w_qkv"] = jnp.concatenate([ab[:, :, :W_A] * DH_A ** -0.5, ab[:, :, W_A:3 * W_A]], axis=-1).astype(BF16)
    w["ab_w_xg"] = ab[:, :, 3 * W_A:].astype(BF16)
    w["ab_w_out"] = p["ab_w_out"].astype(BF16)
    qk = H_C * 2 * DK_C
    cw = p["c_w_in"]
    w["c_w_in"] = jnp.concatenate([cw[:, :, :qk] * (DK_C ** -0.5 * LOG2E), cw[:, :, qk:]], axis=-1).astype(BF16)
    w["c_w_out"] = p["c_w_out"].astype(BF16)
    w["x_wq"] = (p["x_wq"] * DH_X ** -0.5).astype(BF16)
    w["x_wkv"] = p["x_wkv"].astype(BF16)
    w["x_wo"] = p["x_wo"].astype(BF16)
    for n in ("moe_w_gate", "moe_w_up", "moe_w_down"):
        w[n] = p[n].astype(BF16)
    lv = p["c_lam"].astype(F32)
    w["c_lam_scalar"] = [jnp.exp(jnp.sum(lv[i, 0] * lv[i, 1])) - jnp.exp(jnp.sum(lv[i, 2] * lv[i, 3]))
                         + _lambda_init(2 * i + 1) for i in range(lv.shape[0])]
    return w


def _trunk(x3, mem3, w):
    B, S, D = x3.shape
    x = x3.reshape(B * S, D)
    mem = mem3.reshape(B * N_MEM, D)
    table = w["rel_bias"]
    for l in range(DEPTH):
        i = l // 2
        if l % 2 == 0:
            zqkv = norm_matmul(x, w["norm_mix"][l], w["ab_w_qkv"][i], BF16, 1024, 1024)
            xg = norm_matmul(x, w["norm_mix"][l], w["ab_w_xg"][i], F32, 1024, 1024)
            res = [dilated_attention(zqkv, table[:, :H_A], B, S, dil) for _, dil in DILATED_PATTERNS]
            attn = merge_patterns([r[0] for r in res], [r[1] for r in res], 1024)
            rec = rg_lru(xg, w["ab_conv_w"][i], w["ab_conv_b"][i], w["lru_wa"][i], w["lru_ba"][i],
                         w["lru_wx"][i], w["lru_bx"][i], w["lru_lam"][i], B, S)
            x = mix_out(attn, rec, w["ab_w_out"][i], x, 512)
        else:
            z = norm_matmul(x, w["norm_mix"][l], w["c_w_in"][i], BF16, 1024, 1024)
            o = diff_attention(z, table[:, H_A:], w["c_lam_scalar"][i], w["c_subln"][i], _lambda_init(l), B, S)
            x = matmul_residual(o, w["c_w_out"][i], x, 512)
        kvn = norm_matmul(mem, w["norm_mem"][l], w["x_wkv"][l], BF16, 1024, 1024)
        x = cross_attention(x, w["norm_cross"][l], w["x_wq"][l], kvn, w["x_wo"][l], B, S, 512)
        x = hier_moe(x, w["norm_ffn"][l], w["moe_wg"][l], w["moe_bg"][l], w["moe_we"][l], w["moe_be"][l],
                     w["moe_w_gate"][l], w["moe_w_up"][l], w["moe_w_down"][l])
    return final_norm(x, w["norm_final"], 1024).reshape(B, S, D)


def kernel(x_prompt, x_sample, mem_prompt, mem_sample, rel_bias, ab_w_in, ab_conv_w, ab_conv_b, lru_wa, lru_ba, lru_wx, lru_bx, lru_lam, ab_w_out, c_w_in, c_lam, c_subln, c_w_out, norm_mix, norm_cross, norm_mem, x_wq, x_wkv, x_wo, norm_ffn, moe_wg, moe_bg, moe_we, moe_be, moe_w_gate, moe_w_up, moe_w_down, norm_final):
    w = _prep_weights(dict(
        rel_bias=rel_bias, ab_w_in=ab_w_in, ab_conv_w=ab_conv_w, ab_conv_b=ab_conv_b, lru_wa=lru_wa, lru_ba=lru_ba,
        lru_wx=lru_wx, lru_bx=lru_bx, lru_lam=lru_lam, ab_w_out=ab_w_out, c_w_in=c_w_in, c_lam=c_lam,
        c_subln=c_subln, c_w_out=c_w_out, norm_mix=norm_mix, norm_cross=norm_cross, norm_mem=norm_mem, x_wq=x_wq,
        x_wkv=x_wkv, x_wo=x_wo, norm_ffn=norm_ffn, moe_wg=moe_wg, moe_bg=moe_bg, moe_we=moe_we, moe_be=moe_be,
        moe_w_gate=moe_w_gate, moe_w_up=moe_w_up, moe_w_down=moe_w_down, norm_final=norm_final))
    return _trunk(x_prompt, mem_prompt, w), _trunk(x_sample, mem_sample, w)
```

```python
import functools
import math

import jax
import jax.numpy as jnp
from jax import lax
from jax.experimental import pallas as pl
from jax.experimental.pallas import tpu as pltpu

F32 = jnp.float32
BF16 = jnp.bfloat16

D_MODEL = 2048
DEPTH = 4
H_A, DH_A = 16, 64
W_A = H_A * DH_A
DILATED_PATTERNS = ((128, 1), (512, 4), (2048, 16))
D_RNN = 1024
LRU_BLOCKS, LRU_BW = 8, 128
CONV_W = 4
LRU_C = 8.0
H_C, DK_C, DV_C = 8, 128, 256
NUM_BUCKETS, MAX_DISTANCE = 32, 2048
H_X, DH_X, N_MEM = 4, 128, 256
N_GROUPS, EXPERTS_PER_GROUP = 4, 4
N_EXPERTS = N_GROUPS * EXPERTS_PER_GROUP
D_FF = 1024
EPS = 1e-6
NEG = -1e30
LOG2E = math.log2(math.e)

LANES = 128
VMEM_LIMIT = 56 * 1024 * 1024


def _params(sem):
    return pltpu.CompilerParams(dimension_semantics=sem, vmem_limit_bytes=VMEM_LIMIT)


def _rms(x, g):
    return x * lax.rsqrt(jnp.mean(x * x, axis=-1, keepdims=True) + EPS) * g


def _norm_matmul_kernel(x_ref, g_ref, w_ref, o_ref, h_ref):
    @pl.when(pl.program_id(1) == 0)
    def _():
        h_ref[...] = _rms(x_ref[...], g_ref[...]).astype(BF16)

    o_ref[...] = jnp.dot(h_ref[...], w_ref[...], preferred_element_type=F32).astype(o_ref.dtype)


def norm_matmul(x, g, w, out_dtype, tm, tn):
    T, D = x.shape
    N = w.shape[1]
    tm, tn = min(tm, T), min(tn, N)
    assert T % tm == 0 and N % tn == 0
    return pl.pallas_call(
        _norm_matmul_kernel,
        grid=(T // tm, N // tn),
        in_specs=[pl.BlockSpec((tm, D), lambda i, j: (i, 0)),
                  pl.BlockSpec((1, D), lambda i, j: (0, 0)),
                  pl.BlockSpec((D, tn), lambda i, j: (0, j))],
        out_specs=pl.BlockSpec((tm, tn), lambda i, j: (i, j)),
        out_shape=jax.ShapeDtypeStruct((T, N), out_dtype),
        scratch_shapes=[pltpu.VMEM((tm, D), BF16)],
        compiler_params=_params(("parallel", "arbitrary")),
        name="norm_matmul",
    )(x, g.reshape(1, D), w)


def _mix_out_kernel(a1_ref, a2_ref, w_ref, x_ref, o_ref):
    k1 = a1_ref.shape[1]
    acc = jnp.dot(a1_ref[...], w_ref[0:k1, :], preferred_element_type=F32)
    acc += jnp.dot(a2_ref[...], w_ref[k1:, :], preferred_element_type=F32)
    o_ref[...] = x_ref[...] + acc


def mix_out(a1, a2, w, x, tm):
    T, D = x.shape
    K1, K2 = a1.shape[1], a2.shape[1]
    tm = min(tm, T)
    assert T % tm == 0
    return pl.pallas_call(
        _mix_out_kernel,
        grid=(T // tm,),
        in_specs=[pl.BlockSpec((tm, K1), lambda i: (i, 0)),
                  pl.BlockSpec((tm, K2), lambda i: (i, 0)),
                  pl.BlockSpec((K1 + K2, D), lambda i: (0, 0)),
                  pl.BlockSpec((tm, D), lambda i: (i, 0))],
        out_specs=pl.BlockSpec((tm, D), lambda i: (i, 0)),
        out_shape=jax.ShapeDtypeStruct((T, D), F32),
        compiler_params=_params(("parallel",)),
        name="mix_out",
    )(a1, a2, w, x)


def _t5_bucket(rel):
    half = NUM_BUCKETS // 2
    exact = half // 2
    n = jnp.abs(rel)
    large = exact + (jnp.log(jnp.maximum(n, 1).astype(F32) / exact)
                     / math.log(MAX_DISTANCE / exact) * (half - exact)).astype(jnp.int32)
    large = jnp.minimum(large, half - 1)
    return (rel > 0).astype(jnp.int32) * half + jnp.where(n < exact, n, large)


DIL_RADIUS = 64


def _dilated_bias(table, dil, tq):
    win = tq + 2 * DIL_RADIUS
    offs = jnp.array((0, -DIL_RADIUS, -2 * DIL_RADIUS), jnp.int32)
    rel = offs[:, None] - (tq - 1) + jnp.arange(tq + win, dtype=jnp.int32)[None, :]
    w = jnp.where((jnp.abs(rel) <= DIL_RADIUS)[..., None], table[_t5_bucket(rel * dil)].astype(F32), NEG)
    return _toeplitz(w.transpose(0, 2, 1), tq, win)


def _toeplitz(w, tq, tk):
    wd = tq + tk
    lead = w.shape[:-1]
    m = jnp.broadcast_to(w[..., None, :], lead + (tq, wd)).reshape(lead + (tq * wd,))
    m = m[..., :tq * (wd - 1)].reshape(lead + (tq, wd - 1))
    return m[..., tq - 1:tq - 1 + tk]


def _dilated_kernel(q_ref, k_ref, v_ref, b_ref, o_ref, lse_ref, *, L, tq, unroll):
    nq = L // tq
    win = tq + 2 * DIL_RADIUS
    head0 = lax.broadcasted_iota(jnp.int32, (1, LANES), 1) < DH_A

    def body(i, carry):
        i0 = pl.multiple_of(i * tq, tq)
        start = pl.multiple_of(jnp.clip(i0 - DIL_RADIUS, 0, L - win), DIL_RADIUS)
        var = jnp.where(i == 0, 0, jnp.where(i == nq - 1, 2, 1))
        q = q_ref[0, pl.ds(i0, tq), :]
        k = k_ref[0, pl.ds(start, win), :]
        v = v_ref[0, pl.ds(start, win), :]
        outs, lses = [], []
        for h in range(2):
            qh = jnp.where(head0 if h == 0 else jnp.logical_not(head0), q, jnp.zeros_like(q))
            s = lax.dot_general(qh, k, (((1,), (1,)), ((), ())), preferred_element_type=F32)
            s = s + b_ref[var, h]
            m = jnp.max(s, axis=-1, keepdims=True)
            p = jnp.exp(s - m)
            l = jnp.sum(p, axis=-1, keepdims=True)
            outs.append(jnp.dot(p.astype(BF16), v, preferred_element_type=F32) / l)
            lses.append(m + jnp.log(l))
        o_ref[0, pl.ds(i0, tq), :] = jnp.where(head0, outs[0], outs[1]).astype(o_ref.dtype)
        lse_ref[0, pl.ds(i0, tq), :] = jnp.where(head0, lses[0], lses[1])
        return carry

    lax.fori_loop(0, nq, body, 0, unroll=min(unroll, nq))


def dilated_attention(zqkv, table, B, S, dil, tq=128, unroll=8):
    L = S // dil
    assert S % dil == 0 and L % tq == 0 and L >= tq + 2 * DIL_RADIUS
    zv = zqkv.reshape(B, L, dil * 3 * W_A)
    nb = 3 * W_A // LANES
    hb = W_A // LANES
    bias = _dilated_bias(table, dil, tq)
    win = tq + 2 * DIL_RADIUS
    o, lse = pl.pallas_call(
        functools.partial(_dilated_kernel, L=L, tq=tq, unroll=unroll),
        grid=(B, dil, hb),
        in_specs=[pl.BlockSpec((1, L, LANES), lambda b, r, h: (b, 0, r * nb + h)),
                  pl.BlockSpec((1, L, LANES), lambda b, r, h: (b, 0, r * nb + hb + h)),
                  pl.BlockSpec((1, L, LANES), lambda b, r, h: (b, 0, r * nb + 2 * hb + h)),
                  pl.BlockSpec((3, 2, tq, win), lambda b, r, h: (0, h, 0, 0))],
        out_specs=[pl.BlockSpec((1, L, LANES), lambda b, r, h: (b, 0, r * hb + h)),
                   pl.BlockSpec((1, L, LANES), lambda b, r, h: (b, 0, r * hb + h))],
        out_shape=[jax.ShapeDtypeStruct((B, L, dil * W_A), BF16),
                   jax.ShapeDtypeStruct((B, L, dil * W_A), F32)],
        compiler_params=_params(("parallel", "parallel", "parallel")),
        name=f"dilated_attn_d{dil}",
    )(zv, zv, zv, bias)
    return o.reshape(B * S, W_A), lse.reshape(B * S, W_A)


def _merge_kernel(o1, o2, o3, l1, l2, l3, out_ref):
    a, b, c = l1[...], l2[...], l3[...]
    m = jnp.maximum(jnp.maximum(a, b), c)
    ea, eb, ec = jnp.exp(a - m), jnp.exp(b - m), jnp.exp(c - m)
    num = ea * o1[...].astype(F32) + eb * o2[...].astype(F32) + ec * o3[...].astype(F32)
    out_ref[...] = (num / (ea + eb + ec)).astype(out_ref.dtype)


def merge_patterns(os_, lses, tm):
    T, W = os_[0].shape
    tm = min(tm, T)
    spec = pl.BlockSpec((tm, W), lambda i: (i, 0))
    return pl.pallas_call(
        _merge_kernel,
        grid=(T // tm,),
        in_specs=[spec] * 6,
        out_specs=spec,
        out_shape=jax.ShapeDtypeStruct((T, W), BF16),
        compiler_params=_params(("parallel",)),
        name="merge_patterns",
    )(*os_, *lses)


def _lru_kernel(xr_ref, gr_ref, cw_ref, cb_ref, wa_ref, ba_ref, wx_ref, bx_ref, lc_ref, o_ref, hf_ref, *, S, C):
    nC = S // C
    row = lax.broadcasted_iota(jnp.int32, (C, LANES), 0)
    E = C + 16

    def conv_chunk(c):
        t0 = pl.multiple_of(c * C, C)
        cur = xr_ref[0, pl.ds(t0, C), :]
        pstart = pl.multiple_of(jnp.maximum(t0 - 8, 0), 8)
        nstart = pl.multiple_of(jnp.minimum(t0 + C, S - 8), 8)
        prev8 = jnp.where(c > 0, xr_ref[0, pl.ds(pstart, 8), :], 0.0)
        next8 = jnp.where(c < nC - 1, xr_ref[0, pl.ds(nstart, 8), :], 0.0)
        ext = jnp.concatenate([prev8, cur, next8], axis=0)
        xc = cb_ref[...] + ext[8:8 + C] * cw_ref[1:2, :]
        xc += pltpu.roll(ext, 1, 0)[8:8 + C] * cw_ref[0:1, :]
        xc += pltpu.roll(ext, E - 1, 0)[8:8 + C] * cw_ref[2:3, :]
        xc += pltpu.roll(ext, E - 2, 0)[8:8 + C] * cw_ref[3:4, :]
        return t0, xc

    def gates(xc, d):
        xb = xc.astype(BF16)
        r = jax.nn.sigmoid(jnp.dot(xb, wa_ref[d], preferred_element_type=F32) + ba_ref[d])
        i = jax.nn.sigmoid(jnp.dot(xb, wx_ref[d], preferred_element_type=F32) + bx_ref[d])
        log_a = lc_ref[d] * r
        a = jnp.exp(log_a)
        u = jnp.sqrt(-jnp.tanh(log_a) * (a * a + 1.0)) * (i * xc)
        return a, u

    def scan(a, u, reverse):
        k = 1
        while k < C:
            shift = C - k if reverse else k
            valid = (row < C - k) if reverse else (row >= k)
            u = u + a * jnp.where(valid, pltpu.roll(u, shift, 0), 0.0)
            a = a * jnp.where(valid, pltpu.roll(a, shift, 0), 1.0)
            k *= 2
        return a, u

    def fwd(c, carry):
        t0, xc = conv_chunk(c)
        a, u = scan(*gates(xc, 0), False)
        h = u + a * carry
        hf_ref[pl.ds(t0, C), :] = h
        return h[C - 1:C, :]

    lax.fori_loop(0, nC, fwd, jnp.zeros((1, LANES), F32))

    def bwd(j, carry):
        t0, xc = conv_chunk(nC - 1 - j)
        a, u = scan(*gates(xc, 1), True)
        h = u + a * carry
        g = jax.nn.gelu(gr_ref[0, pl.ds(t0, C), :])
        o_ref[0, pl.ds(t0, C), :] = (g * (hf_ref[pl.ds(t0, C), :] + h)).astype(o_ref.dtype)
        return h[0:1, :]

    lax.fori_loop(0, nC, bwd, jnp.zeros((1, LANES), F32))


def rg_lru(xg, conv_w, conv_b, wa, ba, wx, bx, lam, B, S, C=128):
    assert S % C == 0
    xv = xg.reshape(B, S, 2 * D_RNN)
    lc = (-LRU_C * jax.nn.softplus(-lam)).reshape(2, LRU_BLOCKS, 1, LRU_BW)
    vec = lambda t: t.reshape(2, LRU_BLOCKS, 1, LRU_BW)
    wspec = pl.BlockSpec((2, None, LRU_BW, LRU_BW), lambda b, n: (0, n, 0, 0))
    vspec = pl.BlockSpec((2, None, 1, LRU_BW), lambda b, n: (0, n, 0, 0))
    out = pl.pallas_call(
        functools.partial(_lru_kernel, S=S, C=C),
        grid=(B, LRU_BLOCKS),
        in_specs=[pl.BlockSpec((1, S, LRU_BW), lambda b, n: (b, 0, n)),
                  pl.BlockSpec((1, S, LRU_BW), lambda b, n: (b, 0, LRU_BLOCKS + n)),
                  pl.BlockSpec((CONV_W, LRU_BW), lambda b, n: (0, n)),
                  pl.BlockSpec((1, LRU_BW), lambda b, n: (0, n)),
                  wspec, vspec, wspec, vspec, vspec],
        out_specs=pl.BlockSpec((1, S, LRU_BW), lambda b, n: (b, 0, n)),
        out_shape=jax.ShapeDtypeStruct((B, S, D_RNN), BF16),
        scratch_shapes=[pltpu.VMEM((S, LRU_BW), F32)],
        compiler_params=_params(("parallel", "parallel")),
        name="rg_lru",
    )(xv, xv, conv_w, conv_b.reshape(1, D_RNN), wa.astype(BF16), vec(ba), wx.astype(BF16), vec(bx), lc)
    return out.reshape(B * S, D_RNN)


BIAS_SATURATION = MAX_DISTANCE // 2 + 1


def _diff_bias(table, tq, tk, e_lo, e_hi):
    rel = (jnp.arange(e_lo, e_hi + 1, dtype=jnp.int32)[:, None] * tk - (tq - 1)
           + jnp.arange(tq + tk, dtype=jnp.int32)[None, :])
    return _toeplitz(table[_t5_bucket(rel)].astype(F32).transpose(2, 0, 1), tq, tk)


def _lane_tile(x, n):
    return x if n == 1 else jnp.concatenate([x] * n, axis=1)


def _diff_kernel(lam_ref, cb_ref, q_ref, k_ref, v_ref, b_ref, g_ref, o_ref, m_sc, l_sc, acc_sc, *, nk, e_lo, e_hi, mult, rs,
                 unroll):
    kv = pl.program_id(3)
    t = q_ref.shape[1]
    tk = k_ref.shape[1]
    e = kv - (t // tk) * pl.program_id(2)

    @pl.when(kv == 0)
    def _():
        m_sc[...] = jnp.full(m_sc.shape, -jnp.inf, F32)
        l_sc[...] = jnp.zeros(l_sc.shape, F32)
        acc_sc[...] = jnp.zeros(acc_sc.shape, F32)

    def sub(r, carry, *, const_bias):
        r0 = pl.multiple_of(r * rs, rs)
        rows = pl.ds(r0, rs)
        q = q_ref[0, rows, :]
        bias = cb_ref[pl.program_id(1), (e > 0).astype(jnp.int32)] if const_bias else b_ref[0, 0, rows, :]
        for mi in range(2):
            sl = slice(mi * DK_C, (mi + 1) * DK_C)
            s = lax.dot_general(q[:, sl], k_ref[0, :, sl], (((1,), (1,)), ((), ())), preferred_element_type=F32)
            if not const_bias:
                s = s + bias
            m_cur = jnp.max(s, axis=-1, keepdims=True)
            m_prev = m_sc[mi, rows, :]
            m_new = jnp.maximum(m_prev, m_cur + bias if const_bias else m_cur)
            alpha = jnp.exp2(m_prev - m_new)
            p = jnp.exp2(s - _lane_tile(m_new - bias if const_bias else m_new, tk // LANES))
            l_sc[mi, rows, :] = alpha * l_sc[mi, rows, :] + jnp.sum(p, axis=-1, keepdims=True)
            acc_sc[mi, rows, :] = (_lane_tile(alpha, DV_C // LANES) * acc_sc[mi, rows, :]
                                   + jnp.dot(p.astype(BF16), v_ref[0], preferred_element_type=F32))
            m_sc[mi, rows, :] = m_new
        return carry

    far = jnp.logical_or(e >= e_hi, e <= e_lo)
    for const_bias in (False, True):
        @pl.when(far == const_bias)
        def _():
            lax.fori_loop(0, t // rs, functools.partial(sub, const_bias=const_bias), 0,
                          unroll=min(unroll, t // rs))

    @pl.when(kv == nk - 1)
    def _():
        rep = lambda x: _lane_tile(x, DV_C // LANES)
        o = acc_sc[0] / rep(l_sc[0]) - lam_ref[0, 0] * (acc_sc[1] / rep(l_sc[1]))
        o_ref[0] = (_rms(o, g_ref[...]) * mult).astype(o_ref.dtype)


def diff_attention(z, table, lam, subln, lambda_init, B, S, tq=1024, tk=512, rs=128, unroll=8):
    tq, tk = min(tq, S), min(tk, S)
    assert S % tq == 0 and S % tk == 0 and tq % tk == 0 and tq % rs == 0
    u = tq // tk
    e_hi = -(-(BIAS_SATURATION + tq - 1) // tk)
    e_lo = -(-(-(BIAS_SATURATION + tk - 1) // tk))
    bias = _diff_bias(table, tq, tk, e_lo, e_hi) * LOG2E
    zv = z.reshape(B, S, z.shape[1])
    blk = 2 * DK_C
    assert blk == DV_C
    kb, vb = H_C, 2 * H_C
    out = pl.pallas_call(
        functools.partial(_diff_kernel, nk=S // tk, e_lo=e_lo, e_hi=e_hi, mult=1.0 - lambda_init, rs=rs, unroll=unroll),
        grid=(B, H_C, S // tq, S // tk),
        in_specs=[pl.BlockSpec(memory_space=pltpu.SMEM),
                  pl.BlockSpec(memory_space=pltpu.SMEM),
                  pl.BlockSpec((1, tq, blk), lambda b, h, i, j: (b, i, h)),
                  pl.BlockSpec((1, tk, blk), lambda b, h, i, j: (b, j, kb + h)),
                  pl.BlockSpec((1, tk, DV_C), lambda b, h, i, j: (b, j, vb + h)),
                  pl.BlockSpec((1, 1, tq, tk), lambda b, h, i, j: (h, jnp.clip(j - u * i, e_lo, e_hi) - e_lo, 0, 0)),
                  pl.BlockSpec((1, DV_C), lambda b, h, i, j: (0, 0))],
        out_specs=pl.BlockSpec((1, tq, DV_C), lambda b, h, i, j: (b, i, h)),
        out_shape=jax.ShapeDtypeStruct((B, S, H_C * DV_C), BF16),
        scratch_shapes=[pltpu.VMEM((2, tq, LANES), F32), pltpu.VMEM((2, tq, LANES), F32),
                        pltpu.VMEM((2, tq, DV_C), F32)],
        compiler_params=_params(("parallel", "parallel", "parallel", "arbitrary")),
        name="diff_attn",
    )(lam.reshape(1, 1), bias[:, (0, e_hi - e_lo), 0, 0], zv, zv, zv, bias, subln.reshape(1, DV_C))
    return out.reshape(B * S, H_C * DV_C)


def _matmul_res_kernel(a_ref, w_ref, x_ref, o_ref):
    o_ref[...] = x_ref[...] + jnp.dot(a_ref[...], w_ref[...], preferred_element_type=F32)


def matmul_residual(a, w, x, tm):
    T, D = x.shape
    K = a.shape[1]
    tm = min(tm, T)
    assert T % tm == 0
    return pl.pallas_call(
        _matmul_res_kernel,
        grid=(T // tm,),
        in_specs=[pl.BlockSpec((tm, K), lambda i: (i, 0)),
                  pl.BlockSpec((K, D), lambda i: (0, 0)),
                  pl.BlockSpec((tm, D), lambda i: (i, 0))],
        out_specs=pl.BlockSpec((tm, D), lambda i: (i, 0)),
        out_shape=jax.ShapeDtypeStruct((T, D), F32),
        compiler_params=_params(("parallel",)),
        name="matmul_residual",
    )(a, w, x)


def _cross_kernel(x_ref, g_ref, wq_ref, kv_ref, wo_ref, o_ref):
    x = x_ref[0]
    h = _rms(x, g_ref[...]).astype(BF16)
    q = jnp.dot(h, wq_ref[...], preferred_element_type=F32).astype(BF16)
    kv = kv_ref[0]
    hw = H_X * DH_X
    outs = []
    for hd in range(H_X):
        sl = slice(hd * DH_X, (hd + 1) * DH_X)
        s = lax.dot_general(q[:, sl], kv[:, sl], (((1,), (1,)), ((), ())), preferred_element_type=F32)
        m = jnp.max(s, axis=-1, keepdims=True)
        p = jnp.exp(s - m)
        l = jnp.sum(p, axis=-1, keepdims=True)
        vh = kv[:, hw + hd * DH_X: hw + (hd + 1) * DH_X]
        outs.append((jnp.dot(p.astype(BF16), vh, preferred_element_type=F32) / l).astype(BF16))
    o = jnp.concatenate(outs, axis=-1)
    o_ref[0] = x + jnp.dot(o, wo_ref[...], preferred_element_type=F32)


def cross_attention(x, g, wq, kvn, wo, B, S, tm):
    D = x.shape[1]
    tm = min(tm, S)
    assert S % tm == 0
    hw = H_X * DH_X
    out = pl.pallas_call(
        _cross_kernel,
        grid=(B, S // tm),
        in_specs=[pl.BlockSpec((1, tm, D), lambda b, i: (b, i, 0)),
                  pl.BlockSpec((1, D), lambda b, i: (0, 0)),
                  pl.BlockSpec((D, hw), lambda b, i: (0, 0)),
                  pl.BlockSpec((1, N_MEM, 2 * hw), lambda b, i: (b, 0, 0)),
                  pl.BlockSpec((hw, D), lambda b, i: (0, 0))],
        out_specs=pl.BlockSpec((1, tm, D), lambda b, i: (b, i, 0)),
        out_shape=jax.ShapeDtypeStruct((B, S, D), F32),
        compiler_params=_params(("parallel", "parallel")),
        name="cross_attn",
    )(x.reshape(B, S, D), g.reshape(1, D), wq, kvn.reshape(B, N_MEM, 2 * hw), wo)
    return out.reshape(B * S, D)


ROUTE_BIG = 1 << 20


def _router_kernel(x_ref, g_ref, w_ref, b_ref, o_ref):
    h = _rms(x_ref[...], g_ref[...])
    lg = jnp.dot(h, w_ref[...], preferred_element_type=F32, precision=lax.Precision.HIGHEST) + b_ref[...]
    lane = lax.broadcasted_iota(jnp.int32, lg.shape, 1)
    first = lambda mask: jnp.min(jnp.where(mask, lane, ROUTE_BIG), axis=-1, keepdims=True)
    is_g = lane < N_GROUPS
    gl = jnp.where(is_g, lg, -jnp.inf)
    gmax = jnp.max(gl, axis=-1, keepdims=True)
    gidx = first(gl == gmax)
    pg = 1.0 / jnp.sum(jnp.where(is_g, jnp.exp(lg - gmax), 0.0), axis=-1, keepdims=True)
    e_id = lane - N_GROUPS
    in_grp = (e_id >= 0) & (e_id < N_EXPERTS) & ((e_id // EXPERTS_PER_GROUP) == gidx)
    el = jnp.where(in_grp, lg, -jnp.inf)
    v0 = jnp.max(el, axis=-1, keepdims=True)
    i0 = first(el == v0)
    el2 = jnp.where(lane == i0, -jnp.inf, el)
    v1 = jnp.max(el2, axis=-1, keepdims=True)
    i1 = first(el2 == v1)
    t = jnp.exp(v1 - v0)
    w0 = pg / (1.0 + t)
    w1 = w0 * t
    res = jnp.where(lane == 0, (i0 - N_GROUPS).astype(F32),
                    jnp.where(lane == 1, (i1 - N_GROUPS).astype(F32),
                              jnp.where(lane == 2, w0, jnp.where(lane == 3, w1, 0.0))))
    o_ref[...] = res


def moe_router(x, g, wg, bg, we, be, tm):
    T, D = x.shape
    tm = min(tm, T)
    wr = jnp.zeros((D, LANES), F32).at[:, :N_GROUPS].set(wg).at[:, N_GROUPS:N_GROUPS + N_EXPERTS].set(we)
    br = jnp.zeros((1, LANES), F32).at[0, :N_GROUPS].set(bg).at[0, N_GROUPS:N_GROUPS + N_EXPERTS].set(be)
    return pl.pallas_call(
        _router_kernel,
        grid=(T // tm,),
        in_specs=[pl.BlockSpec((tm, D), lambda i: (i, 0)),
                  pl.BlockSpec((1, D), lambda i: (0, 0)),
                  pl.BlockSpec((D, LANES), lambda i: (0, 0)),
                  pl.BlockSpec((1, LANES), lambda i: (0, 0))],
        out_specs=pl.BlockSpec((tm, LANES), lambda i: (i, 0)),
        out_shape=jax.ShapeDtypeStruct((T, LANES), F32),
        compiler_params=_params(("parallel",)),
        name="moe_router",
    )(x, g.reshape(1, D), wr, br)


GATHER_UNROLL = 8


def _start_row_gather(idx_ref, src_hbm, dst, sem, n):
    def body(r, c):
        pltpu.make_async_copy(src_hbm.at[pl.ds(idx_ref[0, 0, r], 1)], dst.at[pl.ds(r, 1)], sem).start()
        return c

    lax.fori_loop(0, n, body, 0, unroll=GATHER_UNROLL)


def _wait_row_gather(src_hbm, dst, sem, n):
    def body(r, c):
        pltpu.make_async_copy(src_hbm.at[pl.ds(0, 1)], dst.at[pl.ds(0, 1)], sem).wait()
        return c

    lax.fori_loop(0, n, body, 0, unroll=GATHER_UNROLL)


def _expert_kernel(be_ref, nu_ref, tok_ref, tok_next_ref, x_hbm, g_ref, wg_ref, wu_ref, wd_ref, o_ref, xbuf, sem, *, blk):
    i = pl.program_id(0)
    nu = nu_ref[0]
    slot = i % 2

    @pl.when(jnp.logical_and(i == 0, nu > 0))
    def _():
        _start_row_gather(tok_ref, x_hbm, xbuf.at[0], sem.at[0], blk)

    @pl.when(i + 1 < nu)
    def _():
        _start_row_gather(tok_next_ref, x_hbm, xbuf.at[1 - slot], sem.at[1 - slot], blk)

    @pl.when(i < nu)
    def _():
        _wait_row_gather(x_hbm, xbuf.at[slot], sem.at[slot], blk)
        h = _rms(xbuf[slot], g_ref[...]).astype(BF16)
        a = jnp.dot(h, wg_ref[0], preferred_element_type=F32)
        b = jnp.dot(h, wu_ref[0], preferred_element_type=F32)
        hid = (jax.nn.silu(a) * b).astype(BF16)
        o_ref[...] = jnp.dot(hid, wd_ref[0], preferred_element_type=F32)

    @pl.when(i >= nu)
    def _():
        o_ref[...] = jnp.zeros(o_ref.shape, o_ref.dtype)


def moe_experts(x, g, buf_tok, blk_e, n_used, w_gate, w_up, w_down, blk):
    T, D = x.shape
    cap = buf_tok.shape[0]
    n_blk = cap // blk
    tok = buf_tok.reshape(n_blk, 1, blk)
    grid_spec = pltpu.PrefetchScalarGridSpec(
        num_scalar_prefetch=2,
        grid=(n_blk,),
        in_specs=[pl.BlockSpec((1, 1, blk), lambda i, be, nu: (i, 0, 0), memory_space=pltpu.SMEM),
                  pl.BlockSpec((1, 1, blk), lambda i, be, nu: (jnp.minimum(i + 1, n_blk - 1), 0, 0),
                               memory_space=pltpu.SMEM),
                  pl.BlockSpec(memory_space=pl.ANY),
                  pl.BlockSpec((1, D), lambda i, be, nu: (0, 0)),
                  pl.BlockSpec((1, D, D_FF), lambda i, be, nu: (be[i], 0, 0)),
                  pl.BlockSpec((1, D, D_FF), lambda i, be, nu: (be[i], 0, 0)),
                  pl.BlockSpec((1, D_FF, D), lambda i, be, nu: (be[i], 0, 0))],
        out_specs=pl.BlockSpec((blk, D), lambda i, be, nu: (i, 0)),
        scratch_shapes=[pltpu.VMEM((2, blk, D), F32), pltpu.SemaphoreType.DMA((2,))],
    )
    return pl.pallas_call(
        functools.partial(_expert_kernel, blk=blk),
        grid_spec=grid_spec,
        out_shape=jax.ShapeDtypeStruct((cap, D), F32),
        compiler_params=_params(("arbitrary",)),
        name="moe_experts",
    )(blk_e, n_used, tok, tok, x, g.reshape(1, D), w_gate, w_up, w_down)


def _combine_kernel(d_ref, d_next_ref, x_ref, r_ref, y_hbm, g_ref, o_ref, buf, sem, *, tt, final):
    i = pl.program_id(0)
    slot = i % 2

    @pl.when(i == 0)
    def _():
        _start_row_gather(d_ref, y_hbm, buf.at[0], sem.at[0], 2 * tt)

    @pl.when(i + 1 < pl.num_programs(0))
    def _():
        _start_row_gather(d_next_ref, y_hbm, buf.at[1 - slot], sem.at[1 - slot], 2 * tt)

    _wait_row_gather(y_hbm, buf.at[slot], sem.at[slot], 2 * tt)
    r = r_ref[...]
    y = x_ref[...] + r[:, 2:3] * buf[slot, 0:tt, :] + r[:, 3:4] * buf[slot, tt:2 * tt, :]
    o_ref[...] = _rms(y, g_ref[...]) if final else y


def moe_combine(x, routed, dest, y, g_final, tt):
    T, D = x.shape
    n = T // tt
    d = dest.reshape(n, tt, 2).transpose(0, 2, 1).reshape(n, 1, 2 * tt)
    final = g_final is not None
    g = (g_final if final else jnp.ones((D,), F32)).reshape(1, D)
    return pl.pallas_call(
        functools.partial(_combine_kernel, tt=tt, final=final),
        grid=(n,),
        in_specs=[pl.BlockSpec((1, 1, 2 * tt), lambda i: (i, 0, 0), memory_space=pltpu.SMEM),
                  pl.BlockSpec((1, 1, 2 * tt), lambda i: (jnp.minimum(i + 1, n - 1), 0, 0), memory_space=pltpu.SMEM),
                  pl.BlockSpec((tt, D), lambda i: (i, 0)),
                  pl.BlockSpec((tt, LANES), lambda i: (i, 0)),
                  pl.BlockSpec(memory_space=pl.ANY),
                  pl.BlockSpec((1, D), lambda i: (0, 0))],
        out_specs=pl.BlockSpec((tt, D), lambda i: (i, 0)),
        out_shape=jax.ShapeDtypeStruct((T, D), F32),
        scratch_shapes=[pltpu.VMEM((2, 2 * tt, D), F32), pltpu.SemaphoreType.DMA((2,))],
        compiler_params=_params(("arbitrary",)),
        name="moe_combine",
    )(d, d, x, routed, y, g)


def _route_plan(routed, blk):
    T = routed.shape[0]
    e = routed[:, :2].astype(jnp.int32)
    flat_e = e.reshape(-1)
    n_rows = 2 * T
    onehot = (flat_e[:, None] == jnp.arange(N_EXPERTS, dtype=jnp.int32)[None, :]).astype(jnp.int32)
    cum = jnp.cumsum(onehot, axis=0)
    rank = jnp.sum(cum * onehot, axis=1) - 1
    counts = cum[-1]
    padded = (counts + blk - 1) // blk * blk
    pad_end = jnp.cumsum(padded)
    pad_start = pad_end - padded
    dest = pad_start[flat_e] + rank
    n_blk = -(-n_rows // blk) + N_EXPERTS
    cap = n_blk * blk
    flat_t = jnp.repeat(jnp.arange(T, dtype=jnp.int32), 2)
    buf_tok = jnp.zeros((cap,), jnp.int32).at[dest].set(flat_t)
    blk_e = jnp.minimum(jnp.searchsorted(pad_end, jnp.arange(n_blk, dtype=jnp.int32) * blk, side='right'),
                        N_EXPERTS - 1).astype(jnp.int32)
    n_used = (pad_end[-1:] // blk).astype(jnp.int32)
    return buf_tok, dest.reshape(T, 2), blk_e, n_used


def hier_moe(x, g, wg, bg, we, be, w_gate, w_up, w_down, g_final=None, blk=256, tm=512, tt=256):
    routed = moe_router(x, g, wg, bg, we, be, tm)
    buf_tok, dest, blk_e, n_used = _route_plan(routed, blk)
    y = moe_experts(x, g, buf_tok, blk_e, n_used, w_gate, w_up, w_down, blk)
    return moe_combine(x, routed, dest, y, g_final, min(tt, x.shape[0]))


def _lambda_init(layer):
    return 0.8 - 0.6 * math.exp(-0.3 * layer)


def _prep_weights(p):
    w = dict(p)
    ab = p["ab_w_in"]
    w["ab_w_qkv"] = jnp.concatenate([ab[:, :, :W_A] * DH_A ** -0.5, ab[:, :, W_A:3 * W_A]], axis=-1).astype(BF16)
    w["ab_w_xg"] = ab[:, :, 3 * W_A:].astype(BF16)
    w["ab_w_out"] = p["ab_w_out"].astype(BF16)
    qk = H_C * 2 * DK_C
    cw = p["c_w_in"]
    w["c_w_in"] = jnp.concatenate([cw[:, :, :qk] * (DK_C ** -0.5 * LOG2E), cw[:, :, qk:]], axis=-1).astype(BF16)
    w["c_w_out"] = p["c_w_out"].astype(BF16)
    w["x_wq"] = (p["x_wq"] * DH_X ** -0.5).astype(BF16)
    w["x_wkv"] = p["x_wkv"].astype(BF16)
    w["x_wo"] = p["x_wo"].astype(BF16)
    for n in ("moe_w_gate", "moe_w_up", "moe_w_down"):
        w[n] = p[n].astype(BF16)
    lv = p["c_lam"].astype(F32)
    w["c_lam_scalar"] = [jnp.exp(jnp.sum(lv[i, 0] * lv[i, 1])) - jnp.exp(jnp.sum(lv[i, 2] * lv[i, 3]))
                         + _lambda_init(2 * i + 1) for i in range(lv.shape[0])]
    return w


def _trunk(x3, mem3, w):
    B, S, D = x3.shape
    x = x3.reshape(B * S, D)
    mem = mem3.reshape(B * N_MEM, D)
    table = w["rel_bias"]
    for l in range(DEPTH):
        i = l // 2
        if l % 2 == 0:
            zqkv = norm_matmul(x, w["norm_mix"][l], w["ab_w_qkv"][i], BF16, 1024, 1024)
            xg = norm_matmul(x, w["norm_mix"][l], w["ab_w_xg"][i], F32, 1024, 1024)
            res = [dilated_attention(zqkv, table[:, :H_A], B, S, dil) for _, dil in DILATED_PATTERNS]
            attn = merge_patterns([r[0] for r in res], [r[1] for r in res], 1024)
            rec = rg_lru(xg, w["ab_conv_w"][i], w["ab_conv_b"][i], w["lru_wa"][i], w["lru_ba"][i],
                         w["lru_wx"][i], w["lru_bx"][i], w["lru_lam"][i], B, S)
            x = mix_out(attn, rec, w["ab_w_out"][i], x, 512)
        else:
            z = norm_matmul(x, w["norm_mix"][l], w["c_w_in"][i], BF16, 1024, 1024)
            o = diff_attention(z, table[:, H_A:], w["c_lam_scalar"][i], w["c_subln"][i], _lambda_init(l), B, S)
            x = matmul_residual(o, w["c_w_out"][i], x, 512)
        kvn = norm_matmul(mem, w["norm_mem"][l], w["x_wkv"][l], BF16, 1024, 1024)
        x = cross_attention(x, w["norm_cross"][l], w["x_wq"][l], kvn, w["x_wo"][l], B, S, 512)
        x = hier_moe(x, w["norm_ffn"][l], w["moe_wg"][l], w["moe_bg"][l], w["moe_we"][l], w["moe_be"][l],
                     w["moe_w_gate"][l], w["moe_w_up"][l], w["moe_w_down"][l],
                     g_final=w["norm_final"] if l == DEPTH - 1 else None)
    return x.reshape(B, S, D)


def kernel(x_prompt, x_sample, mem_prompt, mem_sample, rel_bias, ab_w_in, ab_conv_w, ab_conv_b, lru_wa, lru_ba, lru_wx, lru_bx, lru_lam, ab_w_out, c_w_in, c_lam, c_subln, c_w_out, norm_mix, norm_cross, norm_mem, x_wq, x_wkv, x_wo, norm_ffn, moe_wg, moe_bg, moe_we, moe_be, moe_w_gate, moe_w_up, moe_w_down, norm_final):
    w = _prep_weights(dict(
        rel_bias=rel_bias, ab_w_in=ab_w_in, ab_conv_w=ab_conv_w, ab_conv_b=ab_conv_b, lru_wa=lru_wa, lru_ba=lru_ba,
        lru_wx=lru_wx, lru_bx=lru_bx, lru_lam=lru_lam, ab_w_out=ab_w_out, c_w_in=c_w_in, c_lam=c_lam,
        c_subln=c_subln, c_w_out=c_w_out, norm_mix=norm_mix, norm_cross=norm_cross, norm_mem=norm_mem, x_wq=x_wq,
        x_wkv=x_wkv, x_wo=x_wo, norm_ffn=norm_ffn, moe_wg=moe_wg, moe_bg=moe_bg, moe_we=moe_we, moe_be=moe_be,
        moe_w_gate=moe_w_gate, moe_w_up=moe_w_up, moe_w_down=moe_w_down, norm_final=norm_final))
    return _trunk(x_prompt, mem_prompt, w), _trunk(x_sample, mem_sample, w)
```

```python
import functools
import math

import jax
import jax.numpy as jnp
from jax import lax
from jax.experimental import pallas as pl
from jax.experimental.pallas import tpu as pltpu

F32 = jnp.float32
BF16 = jnp.bfloat16

D_MODEL = 2048
DEPTH = 4
H_A, DH_A = 16, 64
W_A = H_A * DH_A
DILATED_PATTERNS = ((128, 1), (512, 4), (2048, 16))
D_RNN = 1024
LRU_BLOCKS, LRU_BW = 8, 128
CONV_W = 4
LRU_C = 8.0
H_C, DK_C, DV_C = 8, 128, 256
NUM_BUCKETS, MAX_DISTANCE = 32, 2048
H_X, DH_X, N_MEM = 4, 128, 256
N_GROUPS, EXPERTS_PER_GROUP = 4, 4
N_EXPERTS = N_GROUPS * EXPERTS_PER_GROUP
D_FF = 1024
EPS = 1e-6
NEG = -1e30
LOG2E = math.log2(math.e)

LANES = 128
VMEM_LIMIT = 56 * 1024 * 1024


def _params(sem):
    return pltpu.CompilerParams(dimension_semantics=sem, vmem_limit_bytes=VMEM_LIMIT)


def _rms(x, g):
    return x * lax.rsqrt(jnp.mean(x * x, axis=-1, keepdims=True) + EPS) * g


def _norm_matmul_kernel(x_ref, g_ref, w_ref, o_ref, h_ref):
    @pl.when(pl.program_id(1) == 0)
    def _():
        h_ref[...] = _rms(x_ref[...], g_ref[...]).astype(BF16)

    o_ref[...] = jnp.dot(h_ref[...], w_ref[...], preferred_element_type=F32).astype(o_ref.dtype)


def norm_matmul(x, g, w, out_dtype, tm, tn):
    T, D = x.shape
    N = w.shape[1]
    tm, tn = min(tm, T), min(tn, N)
    assert T % tm == 0 and N % tn == 0
    return pl.pallas_call(
        _norm_matmul_kernel,
        grid=(T // tm, N // tn),
        in_specs=[pl.BlockSpec((tm, D), lambda i, j: (i, 0)),
                  pl.BlockSpec((1, D), lambda i, j: (0, 0)),
                  pl.BlockSpec((D, tn), lambda i, j: (0, j))],
        out_specs=pl.BlockSpec((tm, tn), lambda i, j: (i, j)),
        out_shape=jax.ShapeDtypeStruct((T, N), out_dtype),
        scratch_shapes=[pltpu.VMEM((tm, D), BF16)],
        compiler_params=_params(("parallel", "arbitrary")),
        name="norm_matmul",
    )(x, g.reshape(1, D), w)


def _mix_out_kernel(a1_ref, a2_ref, w_ref, x_ref, o_ref):
    k1 = a1_ref.shape[1]
    acc = jnp.dot(a1_ref[...], w_ref[0:k1, :], preferred_element_type=F32)
    acc += jnp.dot(a2_ref[...], w_ref[k1:, :], preferred_element_type=F32)
    o_ref[...] = x_ref[...] + acc


def mix_out(a1, a2, w, x, tm):
    T, D = x.shape
    K1, K2 = a1.shape[1], a2.shape[1]
    tm = min(tm, T)
    assert T % tm == 0
    return pl.pallas_call(
        _mix_out_kernel,
        grid=(T // tm,),
        in_specs=[pl.BlockSpec((tm, K1), lambda i: (i, 0)),
                  pl.BlockSpec((tm, K2), lambda i: (i, 0)),
                  pl.BlockSpec((K1 + K2, D), lambda i: (0, 0)),
                  pl.BlockSpec((tm, D), lambda i: (i, 0))],
        out_specs=pl.BlockSpec((tm, D), lambda i: (i, 0)),
        out_shape=jax.ShapeDtypeStruct((T, D), F32),
        compiler_params=_params(("parallel",)),
        name="mix_out",
    )(a1, a2, w, x)


def _t5_bucket(rel):
    half = NUM_BUCKETS // 2
    exact = half // 2
    n = jnp.abs(rel)
    large = exact + (jnp.log(jnp.maximum(n, 1).astype(F32) / exact)
                     / math.log(MAX_DISTANCE / exact) * (half - exact)).astype(jnp.int32)
    large = jnp.minimum(large, half - 1)
    return (rel > 0).astype(jnp.int32) * half + jnp.where(n < exact, n, large)


DIL_RADIUS = 64


def _dilated_bias(table, dil, tq):
    win = tq + 2 * DIL_RADIUS
    offs = jnp.array((0, -DIL_RADIUS, -2 * DIL_RADIUS), jnp.int32)
    rel = offs[:, None] - (tq - 1) + jnp.arange(tq + win, dtype=jnp.int32)[None, :]
    w = jnp.where((jnp.abs(rel) <= DIL_RADIUS)[..., None], table[_t5_bucket(rel * dil)].astype(F32), NEG)
    return _toeplitz(w.transpose(0, 2, 1), tq, win)


def _toeplitz_kernel(w_ref, o_ref):
    tq, tk = o_ref.shape[2], o_ref.shape[3]
    x = jnp.broadcast_to(w_ref[0, 0], (tq, w_ref.shape[3]))
    o_ref[0, 0] = pltpu.roll(x, 0, 1, stride=1, stride_axis=0)[:, :tk]


def _toeplitz(w, tq, tk):
    a, b, wd = w.shape
    assert wd == tq + tk and wd % LANES == 0
    w = jnp.roll(w, -(tq - 1), axis=-1).reshape(a, b, 1, wd)
    return pl.pallas_call(
        _toeplitz_kernel,
        grid=(a, b),
        in_specs=[pl.BlockSpec((1, 1, 1, wd), lambda i, j: (i, j, 0, 0))],
        out_specs=pl.BlockSpec((1, 1, tq, tk), lambda i, j: (i, j, 0, 0)),
        out_shape=jax.ShapeDtypeStruct((a, b, tq, tk), F32),
        compiler_params=_params(("parallel", "parallel")),
        name="toeplitz_bias",
    )(w)


PERM_BLOCK = 256


def _residue_perm(dil):
    n_l = PERM_BLOCK // dil
    i = jnp.arange(PERM_BLOCK, dtype=jnp.int32)
    src = (i % n_l) * dil + i // n_l
    return (src[:, None] == i[None, :]).astype(BF16)


def _dilated_fused_kernel(q_ref, k_ref, v_ref, b_ref, p4_ref, p4t_ref, p16_ref, p16t_ref, o_ref,
                          qp, kp, vp, op, lp, osc, lsc, *, S, tq, unroll):
    win = tq + 2 * DIL_RADIUS
    head0 = lax.broadcasted_iota(jnp.int32, (1, LANES), 1) < DH_A

    def attend(qr, kr, vr, g, L, n_sub, out_o, out_l):
        nq = L // tq

        def body(n, carry):
            sub = n // nq
            i = n - sub * nq
            base = sub * L
            i0 = i * tq
            start = jnp.clip(i0 - DIL_RADIUS, 0, L - win)
            var = jnp.where(i == 0, 0, jnp.where(i == nq - 1, 2, 1))
            rows = pl.ds(pl.multiple_of(base + i0, tq), tq)
            krows = pl.ds(pl.multiple_of(base + start, DIL_RADIUS), win)
            q, k, v = qr[rows, :], kr[krows, :], vr[krows, :]
            outs, lses = [], []
            for h in range(2):
                qh = jnp.where(head0 if h == 0 else jnp.logical_not(head0), q, jnp.zeros_like(q))
                s = lax.dot_general(qh, k, (((1,), (1,)), ((), ())), preferred_element_type=F32)
                s = s + b_ref[g, var, h]
                m = jnp.max(s, axis=-1, keepdims=True)
                p = jnp.exp(s - m)
                l = jnp.sum(p, axis=-1, keepdims=True)
                outs.append(jnp.dot(p.astype(BF16), v, preferred_element_type=F32) / l)
                lses.append(m + jnp.log(l))
            out_o[rows, :] = jnp.where(head0, outs[0], outs[1])
            out_l[rows, :] = jnp.where(head0, lses[0], lses[1])
            return carry

        lax.fori_loop(0, n_sub * nq, body, 0, unroll=min(unroll, n_sub * nq))

    def regroup(p_ref, dil):
        n_l, L = PERM_BLOCK // dil, S // dil

        def body(c, carry):
            r0 = pl.multiple_of(c * PERM_BLOCK, PERM_BLOCK)
            x = jnp.concatenate([src[0, pl.ds(r0, PERM_BLOCK), :] for src in (q_ref, k_ref, v_ref)], axis=1)
            y = jnp.dot(p_ref[...], x, preferred_element_type=F32).astype(BF16)
            for j, dst in enumerate((qp, kp, vp)):
                for r in range(dil):
                    dst[pl.ds(pl.multiple_of(r * L + c * n_l, n_l), n_l), :] = (
                        y[r * n_l:(r + 1) * n_l, j * LANES:(j + 1) * LANES])
            return carry

        lax.fori_loop(0, S // PERM_BLOCK, body, 0, unroll=4)

    def merge_back(pt_ref, dil, last):
        n_l, L = PERM_BLOCK // dil, S // dil

        def body(c, carry):
            piece = lambda ref: jnp.concatenate(
                [ref[pl.ds(pl.multiple_of(r * L + c * n_l, n_l), n_l), :] for r in range(dil)], axis=0)
            x = piece(lp)
            x1 = x.astype(BF16)
            x2 = (x - x1.astype(F32)).astype(BF16)
            x3 = (x - x1.astype(F32) - x2.astype(F32)).astype(BF16)
            back = jnp.dot(pt_ref[...], jnp.concatenate([piece(op).astype(BF16), x1, x2, x3], axis=1),
                           preferred_element_type=F32)
            o_g = back[:, :LANES]
            lse_g = back[:, LANES:2 * LANES] + back[:, 2 * LANES:3 * LANES] + back[:, 3 * LANES:]
            rows = pl.ds(pl.multiple_of(c * PERM_BLOCK, PERM_BLOCK), PERM_BLOCK)
            lse0 = lsc[rows, :]
            m = jnp.maximum(lse0, lse_g)
            w0, w1 = jnp.exp(lse0 - m), jnp.exp(lse_g - m)
            den = w0 + w1
            o_new = (w0 * osc[rows, :] + w1 * o_g) / den
            if last:
                o_ref[0, rows, :] = o_new.astype(o_ref.dtype)
            else:
                osc[rows, :] = o_new
                lsc[rows, :] = m + jnp.log(den)
            return carry

        lax.fori_loop(0, S // PERM_BLOCK, body, 0, unroll=4)

    attend(q_ref.at[0], k_ref.at[0], v_ref.at[0], 0, S, 1, osc, lsc)
    regroup(p4_ref, 4)
    attend(qp, kp, vp, 1, S // 4, 4, op, lp)
    merge_back(p4t_ref, 4, False)
    regroup(p16_ref, 16)
    attend(qp, kp, vp, 2, S // 16, 16, op, lp)
    merge_back(p16t_ref, 16, True)


DIL_TQ = 128


def dilated_bias_tiles(table):
    return jnp.stack([_dilated_bias(table, d, DIL_TQ) for _, d in DILATED_PATTERNS])


def dilated_attention_fused(zqkv, bias, B, S, unroll=8):
    tq = DIL_TQ
    dils = tuple(d for _, d in DILATED_PATTERNS)
    assert dils == (1, 4, 16) and S % PERM_BLOCK == 0 and (S // 16) % tq == 0 and S // 16 >= tq + 2 * DIL_RADIUS
    zv = zqkv.reshape(B, S, 3 * W_A)
    hb = W_A // LANES
    win = tq + 2 * DIL_RADIUS
    p4, p16 = _residue_perm(4), _residue_perm(16)
    pspec = pl.BlockSpec((PERM_BLOCK, PERM_BLOCK), lambda b, h: (0, 0))
    out = pl.pallas_call(
        functools.partial(_dilated_fused_kernel, S=S, tq=tq, unroll=unroll),
        grid=(B, hb),
        in_specs=[pl.BlockSpec((1, S, LANES), lambda b, h: (b, 0, h)),
                  pl.BlockSpec((1, S, LANES), lambda b, h: (b, 0, hb + h)),
                  pl.BlockSpec((1, S, LANES), lambda b, h: (b, 0, 2 * hb + h)),
                  pl.BlockSpec((3, 3, 2, tq, win), lambda b, h: (0, 0, h, 0, 0)),
                  pspec, pspec, pspec, pspec],
        out_specs=pl.BlockSpec((1, S, LANES), lambda b, h: (b, 0, h)),
        out_shape=jax.ShapeDtypeStruct((B, S, W_A), BF16),
        scratch_shapes=[pltpu.VMEM((S, LANES), BF16)] * 3 + [pltpu.VMEM((S, LANES), F32)] * 4,
        compiler_params=_params(("parallel", "parallel")),
        name="dilated_attn",
    )(zv, zv, zv, bias, p4, p4.T, p16, p16.T)
    return out.reshape(B * S, W_A)


def _lru_kernel(xr_ref, gr_ref, cw_ref, cb_ref, wa_ref, ba_ref, wx_ref, bx_ref, lc_ref, o_ref, hf_ref, *, S, C):
    nC = S // C
    row = lax.broadcasted_iota(jnp.int32, (C, LANES), 0)
    E = C + 16

    def conv_chunk(c):
        t0 = pl.multiple_of(c * C, C)
        cur = xr_ref[0, pl.ds(t0, C), :]
        pstart = pl.multiple_of(jnp.maximum(t0 - 8, 0), 8)
        nstart = pl.multiple_of(jnp.minimum(t0 + C, S - 8), 8)
        prev8 = jnp.where(c > 0, xr_ref[0, pl.ds(pstart, 8), :], 0.0)
        next8 = jnp.where(c < nC - 1, xr_ref[0, pl.ds(nstart, 8), :], 0.0)
        ext = jnp.concatenate([prev8, cur, next8], axis=0)
        xc = cb_ref[...] + ext[8:8 + C] * cw_ref[1:2, :]
        xc += pltpu.roll(ext, 1, 0)[8:8 + C] * cw_ref[0:1, :]
        xc += pltpu.roll(ext, E - 1, 0)[8:8 + C] * cw_ref[2:3, :]
        xc += pltpu.roll(ext, E - 2, 0)[8:8 + C] * cw_ref[3:4, :]
        return t0, xc

    def gates(xc, d):
        xb = xc.astype(BF16)
        r = jax.nn.sigmoid(jnp.dot(xb, wa_ref[d], preferred_element_type=F32) + ba_ref[d])
        i = jax.nn.sigmoid(jnp.dot(xb, wx_ref[d], preferred_element_type=F32) + bx_ref[d])
        log_a = lc_ref[d] * r
        a = jnp.exp(log_a)
        u = jnp.sqrt(-jnp.tanh(log_a) * (a * a + 1.0)) * (i * xc)
        return a, u

    def scan(a, u, reverse):
        k = 1
        while k < C:
            shift = C - k if reverse else k
            valid = (row < C - k) if reverse else (row >= k)
            u = u + a * jnp.where(valid, pltpu.roll(u, shift, 0), 0.0)
            a = a * jnp.where(valid, pltpu.roll(a, shift, 0), 1.0)
            k *= 2
        return a, u

    def fwd(c, carry):
        t0, xc = conv_chunk(c)
        a, u = scan(*gates(xc, 0), False)
        h = u + a * carry
        hf_ref[pl.ds(t0, C), :] = h
        return h[C - 1:C, :]

    lax.fori_loop(0, nC, fwd, jnp.zeros((1, LANES), F32))

    def bwd(j, carry):
        t0, xc = conv_chunk(nC - 1 - j)
        a, u = scan(*gates(xc, 1), True)
        h = u + a * carry
        g = jax.nn.gelu(gr_ref[0, pl.ds(t0, C), :])
        o_ref[0, pl.ds(t0, C), :] = (g * (hf_ref[pl.ds(t0, C), :] + h)).astype(o_ref.dtype)
        return h[0:1, :]

    lax.fori_loop(0, nC, bwd, jnp.zeros((1, LANES), F32))


def rg_lru(xg, conv_w, conv_b, wa, ba, wx, bx, lam, B, S, C=128):
    assert S % C == 0
    xv = xg.reshape(B, S, 2 * D_RNN)
    lc = (-LRU_C * jax.nn.softplus(-lam)).reshape(2, LRU_BLOCKS, 1, LRU_BW)
    vec = lambda t: t.reshape(2, LRU_BLOCKS, 1, LRU_BW)
    wspec = pl.BlockSpec((2, None, LRU_BW, LRU_BW), lambda b, n: (0, n, 0, 0))
    vspec = pl.BlockSpec((2, None, 1, LRU_BW), lambda b, n: (0, n, 0, 0))
    out = pl.pallas_call(
        functools.partial(_lru_kernel, S=S, C=C),
        grid=(B, LRU_BLOCKS),
        in_specs=[pl.BlockSpec((1, S, LRU_BW), lambda b, n: (b, 0, n)),
                  pl.BlockSpec((1, S, LRU_BW), lambda b, n: (b, 0, LRU_BLOCKS + n)),
                  pl.BlockSpec((CONV_W, LRU_BW), lambda b, n: (0, n)),
                  pl.BlockSpec((1, LRU_BW), lambda b, n: (0, n)),
                  wspec, vspec, wspec, vspec, vspec],
        out_specs=pl.BlockSpec((1, S, LRU_BW), lambda b, n: (b, 0, n)),
        out_shape=jax.ShapeDtypeStruct((B, S, D_RNN), BF16),
        scratch_shapes=[pltpu.VMEM((S, LRU_BW), F32)],
        compiler_params=_params(("parallel", "parallel")),
        name="rg_lru",
    )(xv, xv, conv_w, conv_b.reshape(1, D_RNN), wa.astype(BF16), vec(ba), wx.astype(BF16), vec(bx), lc)
    return out.reshape(B * S, D_RNN)


BIAS_SATURATION = MAX_DISTANCE // 2 + 1


def _diff_bias(table, tq, tk, e_lo, e_hi):
    rel = (jnp.arange(e_lo, e_hi + 1, dtype=jnp.int32)[:, None] * tk - (tq - 1)
           + jnp.arange(tq + tk, dtype=jnp.int32)[None, :])
    return _toeplitz(table[_t5_bucket(rel)].astype(F32).transpose(2, 0, 1), tq, tk)


def _lane_tile(x, n):
    return x if n == 1 else jnp.concatenate([x] * n, axis=1)


def _diff_kernel(lam_ref, cb_ref, q_ref, k_ref, v_ref, b_ref, g_ref, o_ref, m_sc, l_sc, acc_sc, *, nk, e_lo, e_hi, mult, rs,
                 unroll):
    kv = pl.program_id(3)
    t = q_ref.shape[1]
    tk = k_ref.shape[1]
    e = kv - (t // tk) * pl.program_id(2)

    @pl.when(kv == 0)
    def _():
        m_sc[...] = jnp.full(m_sc.shape, -jnp.inf, F32)
        l_sc[...] = jnp.zeros(l_sc.shape, F32)
        acc_sc[...] = jnp.zeros(acc_sc.shape, F32)

    def sub(r, carry, *, const_bias):
        r0 = pl.multiple_of(r * rs, rs)
        rows = pl.ds(r0, rs)
        q = q_ref[0, rows, :]
        bias = cb_ref[pl.program_id(1), (e > 0).astype(jnp.int32)] if const_bias else b_ref[0, 0, rows, :]
        for mi in range(2):
            sl = slice(mi * DK_C, (mi + 1) * DK_C)
            s = lax.dot_general(q[:, sl], k_ref[0, :, sl], (((1,), (1,)), ((), ())), preferred_element_type=F32)
            if not const_bias:
                s = s + bias
            m_cur = jnp.max(s, axis=-1, keepdims=True)
            m_prev = m_sc[mi, rows, :]
            m_new = jnp.maximum(m_prev, m_cur + bias if const_bias else m_cur)
            alpha = jnp.exp2(m_prev - m_new)
            p = jnp.exp2(s - _lane_tile(m_new - bias if const_bias else m_new, tk // LANES))
            l_sc[mi, rows, :] = alpha * l_sc[mi, rows, :] + jnp.sum(p, axis=-1, keepdims=True)
            acc_sc[mi, rows, :] = (_lane_tile(alpha, DV_C // LANES) * acc_sc[mi, rows, :]
                                   + jnp.dot(p.astype(BF16), v_ref[0], preferred_element_type=F32))
            m_sc[mi, rows, :] = m_new
        return carry

    far = jnp.logical_or(e >= e_hi, e <= e_lo)
    for const_bias in (False, True):
        @pl.when(far == const_bias)
        def _():
            lax.fori_loop(0, t // rs, functools.partial(sub, const_bias=const_bias), 0,
                          unroll=min(unroll, t // rs))

    @pl.when(kv == nk - 1)
    def _():
        rep = lambda x: _lane_tile(x, DV_C // LANES)
        o = acc_sc[0] / rep(l_sc[0]) - lam_ref[0, 0] * (acc_sc[1] / rep(l_sc[1]))
        o_ref[0] = (_rms(o, g_ref[...]) * mult).astype(o_ref.dtype)


def diff_bias_tiles(table, tq=1024, tk=512):
    e_hi = -(-(BIAS_SATURATION + tq - 1) // tk)
    e_lo = -(-(-(BIAS_SATURATION + tk - 1) // tk))
    return _diff_bias(table, tq, tk, e_lo, e_hi) * LOG2E, e_lo, e_hi


def diff_attention(z, bias_tiles, lam, subln, lambda_init, B, S, rs=128, unroll=8):
    bias, e_lo, e_hi = bias_tiles
    tq, tk = bias.shape[2:]
    assert S % tq == 0 and S % tk == 0 and tq % tk == 0 and tq % rs == 0
    u = tq // tk
    zv = z.reshape(B, S, z.shape[1])
    blk = 2 * DK_C
    assert blk == DV_C
    kb, vb = H_C, 2 * H_C
    out = pl.pallas_call(
        functools.partial(_diff_kernel, nk=S // tk, e_lo=e_lo, e_hi=e_hi, mult=1.0 - lambda_init, rs=rs, unroll=unroll),
        grid=(B, H_C, S // tq, S // tk),
        in_specs=[pl.BlockSpec(memory_space=pltpu.SMEM),
                  pl.BlockSpec(memory_space=pltpu.SMEM),
                  pl.BlockSpec((1, tq, blk), lambda b, h, i, j: (b, i, h)),
                  pl.BlockSpec((1, tk, blk), lambda b, h, i, j: (b, j, kb + h)),
                  pl.BlockSpec((1, tk, DV_C), lambda b, h, i, j: (b, j, vb + h)),
                  pl.BlockSpec((1, 1, tq, tk), lambda b, h, i, j: (h, jnp.clip(j - u * i, e_lo, e_hi) - e_lo, 0, 0)),
                  pl.BlockSpec((1, DV_C), lambda b, h, i, j: (0, 0))],
        out_specs=pl.BlockSpec((1, tq, DV_C), lambda b, h, i, j: (b, i, h)),
        out_shape=jax.ShapeDtypeStruct((B, S, H_C * DV_C), BF16),
        scratch_shapes=[pltpu.VMEM((2, tq, LANES), F32), pltpu.VMEM((2, tq, LANES), F32),
                        pltpu.VMEM((2, tq, DV_C), F32)],
        compiler_params=_params(("parallel", "parallel", "parallel", "arbitrary")),
        name="diff_attn",
    )(lam.reshape(1, 1), bias[:, (0, e_hi - e_lo), 0, 0], zv, zv, zv, bias, subln.reshape(1, DV_C))
    return out.reshape(B * S, H_C * DV_C)


def _matmul_res_kernel(a_ref, w_ref, x_ref, o_ref):
    o_ref[...] = x_ref[...] + jnp.dot(a_ref[...], w_ref[...], preferred_element_type=F32)


def matmul_residual(a, w, x, tm):
    T, D = x.shape
    K = a.shape[1]
    tm = min(tm, T)
    assert T % tm == 0
    return pl.pallas_call(
        _matmul_res_kernel,
        grid=(T // tm,),
        in_specs=[pl.BlockSpec((tm, K), lambda i: (i, 0)),
                  pl.BlockSpec((K, D), lambda i: (0, 0)),
                  pl.BlockSpec((tm, D), lambda i: (i, 0))],
        out_specs=pl.BlockSpec((tm, D), lambda i: (i, 0)),
        out_shape=jax.ShapeDtypeStruct((T, D), F32),
        compiler_params=_params(("parallel",)),
        name="matmul_residual",
    )(a, w, x)


def _cross_kernel(x_ref, g_ref, wq_ref, kv_ref, wo_ref, o_ref):
    x = x_ref[0]
    h = _rms(x, g_ref[...]).astype(BF16)
    q = jnp.dot(h, wq_ref[...], preferred_element_type=F32).astype(BF16)
    kv = kv_ref[0]
    hw = H_X * DH_X
    outs = []
    for hd in range(H_X):
        sl = slice(hd * DH_X, (hd + 1) * DH_X)
        s = lax.dot_general(q[:, sl], kv[:, sl], (((1,), (1,)), ((), ())), preferred_element_type=F32)
        m = jnp.max(s, axis=-1, keepdims=True)
        p = jnp.exp(s - m)
        l = jnp.sum(p, axis=-1, keepdims=True)
        vh = kv[:, hw + hd * DH_X: hw + (hd + 1) * DH_X]
        outs.append((jnp.dot(p.astype(BF16), vh, preferred_element_type=F32) / l).astype(BF16))
    o = jnp.concatenate(outs, axis=-1)
    o_ref[0] = x + jnp.dot(o, wo_ref[...], preferred_element_type=F32)


def cross_attention(x, g, wq, kvn, wo, B, S, tm):
    D = x.shape[1]
    tm = min(tm, S)
    assert S % tm == 0
    hw = H_X * DH_X
    out = pl.pallas_call(
        _cross_kernel,
        grid=(B, S // tm),
        in_specs=[pl.BlockSpec((1, tm, D), lambda b, i: (b, i, 0)),
                  pl.BlockSpec((1, D), lambda b, i: (0, 0)),
                  pl.BlockSpec((D, hw), lambda b, i: (0, 0)),
                  pl.BlockSpec((1, N_MEM, 2 * hw), lambda b, i: (b, 0, 0)),
                  pl.BlockSpec((hw, D), lambda b, i: (0, 0))],
        out_specs=pl.BlockSpec((1, tm, D), lambda b, i: (b, i, 0)),
        out_shape=jax.ShapeDtypeStruct((B, S, D), F32),
        compiler_params=_params(("parallel", "parallel")),
        name="cross_attn",
    )(x.reshape(B, S, D), g.reshape(1, D), wq, kvn.reshape(B, N_MEM, 2 * hw), wo)
    return out.reshape(B * S, D)


ROUTE_BIG = 1 << 20


def _router_kernel(x_ref, g_ref, w_ref, b_ref, o_ref):
    h = _rms(x_ref[...], g_ref[...])
    lg = jnp.dot(h, w_ref[...], preferred_element_type=F32, precision=lax.Precision.HIGHEST) + b_ref[...]
    lane = lax.broadcasted_iota(jnp.int32, lg.shape, 1)
    first = lambda mask: jnp.min(jnp.where(mask, lane, ROUTE_BIG), axis=-1, keepdims=True)
    is_g = lane < N_GROUPS
    gl = jnp.where(is_g, lg, -jnp.inf)
    gmax = jnp.max(gl, axis=-1, keepdims=True)
    gidx = first(gl == gmax)
    pg = 1.0 / jnp.sum(jnp.where(is_g, jnp.exp(lg - gmax), 0.0), axis=-1, keepdims=True)
    e_id = lane - N_GROUPS
    in_grp = (e_id >= 0) & (e_id < N_EXPERTS) & ((e_id // EXPERTS_PER_GROUP) == gidx)
    el = jnp.where(in_grp, lg, -jnp.inf)
    v0 = jnp.max(el, axis=-1, keepdims=True)
    i0 = first(el == v0)
    el2 = jnp.where(lane == i0, -jnp.inf, el)
    v1 = jnp.max(el2, axis=-1, keepdims=True)
    i1 = first(el2 == v1)
    t = jnp.exp(v1 - v0)
    w0 = pg / (1.0 + t)
    w1 = w0 * t
    res = jnp.where(lane == 0, (i0 - N_GROUPS).astype(F32),
                    jnp.where(lane == 1, (i1 - N_GROUPS).astype(F32),
                              jnp.where(lane == 2, w0, jnp.where(lane == 3, w1, 0.0))))
    o_ref[...] = res


def moe_router(x, g, wg, bg, we, be, tm):
    T, D = x.shape
    tm = min(tm, T)
    wr = jnp.zeros((D, LANES), F32).at[:, :N_GROUPS].set(wg).at[:, N_GROUPS:N_GROUPS + N_EXPERTS].set(we)
    br = jnp.zeros((1, LANES), F32).at[0, :N_GROUPS].set(bg).at[0, N_GROUPS:N_GROUPS + N_EXPERTS].set(be)
    return pl.pallas_call(
        _router_kernel,
        grid=(T // tm,),
        in_specs=[pl.BlockSpec((tm, D), lambda i: (i, 0)),
                  pl.BlockSpec((1, D), lambda i: (0, 0)),
                  pl.BlockSpec((D, LANES), lambda i: (0, 0)),
                  pl.BlockSpec((1, LANES), lambda i: (0, 0))],
        out_specs=pl.BlockSpec((tm, LANES), lambda i: (i, 0)),
        out_shape=jax.ShapeDtypeStruct((T, LANES), F32),
        compiler_params=_params(("parallel",)),
        name="moe_router",
    )(x, g.reshape(1, D), wr, br)


GATHER_UNROLL = 8


def _start_row_gather(idx_ref, src_hbm, dst, sem, n):
    def body(r, c):
        pltpu.make_async_copy(src_hbm.at[pl.ds(idx_ref[0, 0, r], 1)], dst.at[pl.ds(r, 1)], sem).start()
        return c

    lax.fori_loop(0, n, body, 0, unroll=GATHER_UNROLL)


def _wait_row_gather(src_hbm, dst, sem, n):
    def body(r, c):
        pltpu.make_async_copy(src_hbm.at[pl.ds(0, 1)], dst.at[pl.ds(0, 1)], sem).wait()
        return c

    lax.fori_loop(0, n, body, 0, unroll=GATHER_UNROLL)


def _expert_kernel(be_ref, nu_ref, tok_ref, tok_next_ref, x_hbm, g_ref, wg_ref, wu_ref, wd_ref, o_ref, xbuf, sem, *, blk):
    i = pl.program_id(0)
    nu = nu_ref[0]
    slot = i % 2

    @pl.when(jnp.logical_and(i == 0, nu > 0))
    def _():
        _start_row_gather(tok_ref, x_hbm, xbuf.at[0], sem.at[0], blk)

    @pl.when(i < nu)
    def _():
        _wait_row_gather(x_hbm, xbuf.at[slot], sem.at[slot], blk)
        for r in range(blk):
            pltpu.make_async_copy(x_hbm.at[pl.ds(tok_next_ref[0, 0, r], 1)], xbuf.at[1 - slot, pl.ds(r, 1)],
                                  sem.at[1 - slot]).start()
        h = _rms(xbuf[slot], g_ref[...]).astype(BF16)
        a = jnp.dot(h, wg_ref[0], preferred_element_type=F32)
        b = jnp.dot(h, wu_ref[0], preferred_element_type=F32)
        hid = (jax.nn.silu(a) * b).astype(BF16)
        o_ref[...] = jnp.dot(hid, wd_ref[0], preferred_element_type=F32)

    @pl.when(i == nu - 1)
    def _():
        _wait_row_gather(x_hbm, xbuf.at[1 - slot], sem.at[1 - slot], blk)

    @pl.when(i >= nu)
    def _():
        o_ref[...] = jnp.zeros(o_ref.shape, o_ref.dtype)


def moe_experts(x, g, buf_tok, blk_e, n_used, w_gate, w_up, w_down, blk):
    T, D = x.shape
    cap = buf_tok.shape[0]
    n_blk = cap // blk
    tok = buf_tok.reshape(n_blk, 1, blk)
    grid_spec = pltpu.PrefetchScalarGridSpec(
        num_scalar_prefetch=2,
        grid=(n_blk,),
        in_specs=[pl.BlockSpec((1, 1, blk), lambda i, be, nu: (i, 0, 0), memory_space=pltpu.SMEM),
                  pl.BlockSpec((1, 1, blk), lambda i, be, nu: (jnp.minimum(i + 1, jnp.maximum(nu[0] - 1, 0)), 0, 0),
                               memory_space=pltpu.SMEM),
                  pl.BlockSpec(memory_space=pl.ANY),
                  pl.BlockSpec((1, D), lambda i, be, nu: (0, 0)),
                  pl.BlockSpec((1, D, D_FF), lambda i, be, nu: (be[i], 0, 0)),
                  pl.BlockSpec((1, D, D_FF), lambda i, be, nu: (be[i], 0, 0)),
                  pl.BlockSpec((1, D_FF, D), lambda i, be, nu: (be[i], 0, 0))],
        out_specs=pl.BlockSpec((blk, D), lambda i, be, nu: (i, 0)),
        scratch_shapes=[pltpu.VMEM((2, blk, D), F32), pltpu.SemaphoreType.DMA((2,))],
    )
    return pl.pallas_call(
        functools.partial(_expert_kernel, blk=blk),
        grid_spec=grid_spec,
        out_shape=jax.ShapeDtypeStruct((cap, D), F32),
        compiler_params=_params(("arbitrary",)),
        name="moe_experts",
    )(blk_e, n_used, tok, tok, x, g.reshape(1, D), w_gate, w_up, w_down)


def _combine_kernel(d_ref, d_next_ref, x_ref, r_ref, y_hbm, g_ref, o_ref, buf, sem, *, tt, final):
    i = pl.program_id(0)
    slot = i % 2

    @pl.when(i == 0)
    def _():
        _start_row_gather(d_ref, y_hbm, buf.at[0], sem.at[0], 2 * tt)

    @pl.when(i + 1 < pl.num_programs(0))
    def _():
        _start_row_gather(d_next_ref, y_hbm, buf.at[1 - slot], sem.at[1 - slot], 2 * tt)

    _wait_row_gather(y_hbm, buf.at[slot], sem.at[slot], 2 * tt)
    r = r_ref[...]
    y = x_ref[...] + r[:, 2:3] * buf[slot, 0:tt, :] + r[:, 3:4] * buf[slot, tt:2 * tt, :]
    o_ref[...] = _rms(y, g_ref[...]) if final else y


def moe_combine(x, routed, dest, y, g_final, tt):
    T, D = x.shape
    n = T // tt
    d = dest.reshape(n, tt, 2).transpose(0, 2, 1).reshape(n, 1, 2 * tt)
    final = g_final is not None
    g = (g_final if final else jnp.ones((D,), F32)).reshape(1, D)
    return pl.pallas_call(
        functools.partial(_combine_kernel, tt=tt, final=final),
        grid=(n,),
        in_specs=[pl.BlockSpec((1, 1, 2 * tt), lambda i: (i, 0, 0), memory_space=pltpu.SMEM),
                  pl.BlockSpec((1, 1, 2 * tt), lambda i: (jnp.minimum(i + 1, n - 1), 0, 0), memory_space=pltpu.SMEM),
                  pl.BlockSpec((tt, D), lambda i: (i, 0)),
                  pl.BlockSpec((tt, LANES), lambda i: (i, 0)),
                  pl.BlockSpec(memory_space=pl.ANY),
                  pl.BlockSpec((1, D), lambda i: (0, 0))],
        out_specs=pl.BlockSpec((tt, D), lambda i: (i, 0)),
        out_shape=jax.ShapeDtypeStruct((T, D), F32),
        scratch_shapes=[pltpu.VMEM((2, 2 * tt, D), F32), pltpu.SemaphoreType.DMA((2,))],
        compiler_params=_params(("arbitrary",)),
        name="moe_combine",
    )(d, d, x, routed, y, g)


def _route_plan(routed, blk):
    T = routed.shape[0]
    e = routed[:, :2].astype(jnp.int32)
    flat_e = e.reshape(-1)
    n_rows = 2 * T
    onehot = (flat_e[:, None] == jnp.arange(N_EXPERTS, dtype=jnp.int32)[None, :]).astype(jnp.int32)
    cum = jnp.cumsum(onehot, axis=0)
    rank = jnp.sum(cum * onehot, axis=1) - 1
    counts = cum[-1]
    padded = (counts + blk - 1) // blk * blk
    pad_end = jnp.cumsum(padded)
    pad_start = pad_end - padded
    dest = pad_start[flat_e] + rank
    n_blk = -(-n_rows // blk) + N_EXPERTS
    cap = n_blk * blk
    flat_t = jnp.repeat(jnp.arange(T, dtype=jnp.int32), 2)
    buf_tok = jnp.zeros((cap,), jnp.int32).at[dest].set(flat_t)
    blk_e = jnp.minimum(jnp.searchsorted(pad_end, jnp.arange(n_blk, dtype=jnp.int32) * blk, side='right'),
                        N_EXPERTS - 1).astype(jnp.int32)
    n_used = (pad_end[-1:] // blk).astype(jnp.int32)
    return buf_tok, dest.reshape(T, 2), blk_e, n_used


def hier_moe(x, g, wg, bg, we, be, w_gate, w_up, w_down, g_final=None, blk=256, tm=512, tt=256):
    routed = moe_router(x, g, wg, bg, we, be, tm)
    buf_tok, dest, blk_e, n_used = _route_plan(routed, blk)
    y = moe_experts(x, g, buf_tok, blk_e, n_used, w_gate, w_up, w_down, blk)
    return moe_combine(x, routed, dest, y, g_final, min(tt, x.shape[0]))


def _lambda_init(layer):
    return 0.8 - 0.6 * math.exp(-0.3 * layer)


def _prep_weights(p):
    w = dict(p)
    ab = p["ab_w_in"]
    w["ab_w_qkv"] = jnp.concatenate([ab[:, :, :W_A] * DH_A ** -0.5, ab[:, :, W_A:3 * W_A]], axis=-1).astype(BF16)
    w["ab_w_xg"] = ab[:, :, 3 * W_A:].astype(BF16)
    w["ab_w_out"] = p["ab_w_out"].astype(BF16)
    qk = H_C * 2 * DK_C
    cw = p["c_w_in"]
    w["c_w_in"] = jnp.concatenate([cw[:, :, :qk] * (DK_C ** -0.5 * LOG2E), cw[:, :, qk:]], axis=-1).astype(BF16)
    w["c_w_out"] = p["c_w_out"].astype(BF16)
    w["x_wq"] = (p["x_wq"] * DH_X ** -0.5).astype(BF16)
    w["x_wkv"] = p["x_wkv"].astype(BF16)
    w["x_wo"] = p["x_wo"].astype(BF16)
    for n in ("moe_w_gate", "moe_w_up", "moe_w_down"):
        w[n] = p[n].astype(BF16)
    lv = p["c_lam"].astype(F32)
    w["c_lam_scalar"] = [jnp.exp(jnp.sum(lv[i, 0] * lv[i, 1])) - jnp.exp(jnp.sum(lv[i, 2] * lv[i, 3]))
                         + _lambda_init(2 * i + 1) for i in range(lv.shape[0])]
    w["dil_bias"] = dilated_bias_tiles(p["rel_bias"][:, :H_A])
    w["diff_bias"] = diff_bias_tiles(p["rel_bias"][:, H_A:])
    return w


def _trunk(x3, mem3, w):
    B, S, D = x3.shape
    x = x3.reshape(B * S, D)
    mem = mem3.reshape(B * N_MEM, D)
    for l in range(DEPTH):
        i = l // 2
        if l % 2 == 0:
            zqkv = norm_matmul(x, w["norm_mix"][l], w["ab_w_qkv"][i], BF16, 1024, 1024)
            xg = norm_matmul(x, w["norm_mix"][l], w["ab_w_xg"][i], F32, 1024, 1024)
            attn = dilated_attention_fused(zqkv, w["dil_bias"], B, S)
            rec = rg_lru(xg, w["ab_conv_w"][i], w["ab_conv_b"][i], w["lru_wa"][i], w["lru_ba"][i],
                         w["lru_wx"][i], w["lru_bx"][i], w["lru_lam"][i], B, S)
            x = mix_out(attn, rec, w["ab_w_out"][i], x, 512)
        else:
            z = norm_matmul(x, w["norm_mix"][l], w["c_w_in"][i], BF16, 1024, 1024)
            o = diff_attention(z, w["diff_bias"], w["c_lam_scalar"][i], w["c_subln"][i], _lambda_init(l), B, S)
            x = matmul_residual(o, w["c_w_out"][i], x, 512)
        kvn = norm_matmul(mem, w["norm_mem"][l], w["x_wkv"][l], BF16, 1024, 1024)
        x = cross_attention(x, w["norm_cross"][l], w["x_wq"][l], kvn, w["x_wo"][l], B, S, 512)
        x = hier_moe(x, w["norm_ffn"][l], w["moe_wg"][l], w["moe_bg"][l], w["moe_we"][l], w["moe_be"][l],
                     w["moe_w_gate"][l], w["moe_w_up"][l], w["moe_w_down"][l],
                     g_final=w["norm_final"] if l == DEPTH - 1 else None)
    return x.reshape(B, S, D)


def kernel(x_prompt, x_sample, mem_prompt, mem_sample, rel_bias, ab_w_in, ab_conv_w, ab_conv_b, lru_wa, lru_ba, lru_wx, lru_bx, lru_lam, ab_w_out, c_w_in, c_lam, c_subln, c_w_out, norm_mix, norm_cross, norm_mem, x_wq, x_wkv, x_wo, norm_ffn, moe_wg, moe_bg, moe_we, moe_be, moe_w_gate, moe_w_up, moe_w_down, norm_final):
    w = _prep_weights(dict(
        rel_bias=rel_bias, ab_w_in=ab_w_in, ab_conv_w=ab_conv_w, ab_conv_b=ab_conv_b, lru_wa=lru_wa, lru_ba=lru_ba,
        lru_wx=lru_wx, lru_bx=lru_bx, lru_lam=lru_lam, ab_w_out=ab_w_out, c_w_in=c_w_in, c_lam=c_lam,
        c_subln=c_subln, c_w_out=c_w_out, norm_mix=norm_mix, norm_cross=norm_cross, norm_mem=norm_mem, x_wq=x_wq,
        x_wkv=x_wkv, x_wo=x_wo, norm_ffn=norm_ffn, moe_wg=moe_wg, moe_bg=moe_bg, moe_we=moe_we, moe_be=moe_be,
        moe_w_gate=moe_w_gate, moe_w_up=moe_w_up, moe_w_down=moe_w_down, norm_final=norm_final))
    return _trunk(x_prompt, mem_prompt, w), _trunk(x_sample, mem_sample, w)
```

```python
import functools
import math

import jax
import jax.numpy as jnp
from jax import lax
from jax.experimental import pallas as pl
from jax.experimental.pallas import tpu as pltpu

F32 = jnp.float32
BF16 = jnp.bfloat16

D_MODEL = 2048
DEPTH = 4
H_A, DH_A = 16, 64
W_A = H_A * DH_A
DILATED_PATTERNS = ((128, 1), (512, 4), (2048, 16))
D_RNN = 1024
LRU_BLOCKS, LRU_BW = 8, 128
CONV_W = 4
LRU_C = 8.0
H_C, DK_C, DV_C = 8, 128, 256
NUM_BUCKETS, MAX_DISTANCE = 32, 2048
H_X, DH_X, N_MEM = 4, 128, 256
N_GROUPS, EXPERTS_PER_GROUP = 4, 4
N_EXPERTS = N_GROUPS * EXPERTS_PER_GROUP
D_FF = 1024
EPS = 1e-6
NEG = -1e30
LOG2E = math.log2(math.e)

LANES = 128
VMEM_LIMIT = 56 * 1024 * 1024


def _params(sem):
    return pltpu.CompilerParams(dimension_semantics=sem, vmem_limit_bytes=VMEM_LIMIT)


def _rms(x, g):
    return x * lax.rsqrt(jnp.mean(x * x, axis=-1, keepdims=True) + EPS) * g


def _norm_matmul_kernel(x_ref, g_ref, w_ref, o_ref, h_ref):
    @pl.when(pl.program_id(1) == 0)
    def _():
        h_ref[...] = _rms(x_ref[...], g_ref[...]).astype(BF16)

    o_ref[...] = jnp.dot(h_ref[...], w_ref[...], preferred_element_type=F32).astype(o_ref.dtype)


def norm_matmul(x, g, w, out_dtype, tm, tn, col0=0, n_cols=None):
    T, D = x.shape
    N = w.shape[1] - col0 if n_cols is None else n_cols
    tm, tn = min(tm, T), min(tn, N)
    assert T % tm == 0 and N % tn == 0 and col0 % tn == 0
    j0 = col0 // tn
    return pl.pallas_call(
        _norm_matmul_kernel,
        grid=(T // tm, N // tn),
        in_specs=[pl.BlockSpec((tm, D), lambda i, j: (i, 0)),
                  pl.BlockSpec((1, D), lambda i, j: (0, 0)),
                  pl.BlockSpec((D, tn), lambda i, j: (0, j0 + j))],
        out_specs=pl.BlockSpec((tm, tn), lambda i, j: (i, j)),
        out_shape=jax.ShapeDtypeStruct((T, N), out_dtype),
        scratch_shapes=[pltpu.VMEM((tm, D), BF16)],
        compiler_params=_params(("parallel", "arbitrary")),
        name="norm_matmul",
    )(x, g.reshape(1, D), w)


def _mix_out_kernel(a1_ref, a2_ref, w_ref, x_ref, o_ref):
    k1 = a1_ref.shape[1]
    acc = jnp.dot(a1_ref[...], w_ref[0:k1, :], preferred_element_type=F32)
    acc += jnp.dot(a2_ref[...], w_ref[k1:, :], preferred_element_type=F32)
    o_ref[...] = x_ref[...] + acc


def mix_out(a1, a2, w, x, tm):
    T, D = x.shape
    K1, K2 = a1.shape[1], a2.shape[1]
    tm = min(tm, T)
    assert T % tm == 0
    return pl.pallas_call(
        _mix_out_kernel,
        grid=(T // tm,),
        in_specs=[pl.BlockSpec((tm, K1), lambda i: (i, 0)),
                  pl.BlockSpec((tm, K2), lambda i: (i, 0)),
                  pl.BlockSpec((K1 + K2, D), lambda i: (0, 0)),
                  pl.BlockSpec((tm, D), lambda i: (i, 0))],
        out_specs=pl.BlockSpec((tm, D), lambda i: (i, 0)),
        out_shape=jax.ShapeDtypeStruct((T, D), F32),
        compiler_params=_params(("parallel",)),
        name="mix_out",
    )(a1, a2, w, x)


def _t5_bucket(rel):
    half = NUM_BUCKETS // 2
    exact = half // 2
    n = jnp.abs(rel)
    large = exact + (jnp.log(jnp.maximum(n, 1).astype(F32) / exact)
                     / math.log(MAX_DISTANCE / exact) * (half - exact)).astype(jnp.int32)
    large = jnp.minimum(large, half - 1)
    return (rel > 0).astype(jnp.int32) * half + jnp.where(n < exact, n, large)


DIL_RADIUS = 64


def _dilated_bias(table, dil, tq):
    win = tq + 2 * DIL_RADIUS
    offs = jnp.array((0, -DIL_RADIUS, -2 * DIL_RADIUS), jnp.int32)
    rel = offs[:, None] - (tq - 1) + jnp.arange(tq + win, dtype=jnp.int32)[None, :]
    w = jnp.where((jnp.abs(rel) <= DIL_RADIUS)[..., None], table[_t5_bucket(rel * dil)].astype(F32), NEG)
    return _toeplitz(w.transpose(0, 2, 1), tq, win)


def _toeplitz_kernel(w_ref, o_ref):
    tq, tk = o_ref.shape[2], o_ref.shape[3]
    x = jnp.broadcast_to(w_ref[0, 0], (tq, w_ref.shape[3]))
    o_ref[0, 0] = pltpu.roll(x, 0, 1, stride=1, stride_axis=0)[:, :tk]


def _toeplitz(w, tq, tk):
    a, b, wd = w.shape
    assert wd == tq + tk and wd % LANES == 0
    w = jnp.roll(w, -(tq - 1), axis=-1).reshape(a, b, 1, wd)
    return pl.pallas_call(
        _toeplitz_kernel,
        grid=(a, b),
        in_specs=[pl.BlockSpec((1, 1, 1, wd), lambda i, j: (i, j, 0, 0))],
        out_specs=pl.BlockSpec((1, 1, tq, tk), lambda i, j: (i, j, 0, 0)),
        out_shape=jax.ShapeDtypeStruct((a, b, tq, tk), F32),
        compiler_params=_params(("parallel", "parallel")),
        name="toeplitz_bias",
    )(w)


PERM_BLOCK = 256


def _residue_perm(dil):
    n_l = PERM_BLOCK // dil
    i = jnp.arange(PERM_BLOCK, dtype=jnp.int32)
    src = (i % n_l) * dil + i // n_l
    return (src[:, None] == i[None, :]).astype(BF16)


def _dilated_fused_kernel(q_ref, k_ref, v_ref, b_ref, p4_ref, p4t_ref, p16_ref, p16t_ref, o_ref,
                          qp, kp, vp, op, lp, osc, lsc, *, S, tq, unroll):
    win = tq + 2 * DIL_RADIUS
    head0 = lax.broadcasted_iota(jnp.int32, (1, LANES), 1) < DH_A

    def attend(qr, kr, vr, g, L, n_sub, out_o, out_l):
        nq = L // tq

        def body(n, carry):
            sub = n // nq
            i = n - sub * nq
            base = sub * L
            i0 = i * tq
            start = jnp.clip(i0 - DIL_RADIUS, 0, L - win)
            var = jnp.where(i == 0, 0, jnp.where(i == nq - 1, 2, 1))
            rows = pl.ds(pl.multiple_of(base + i0, tq), tq)
            krows = pl.ds(pl.multiple_of(base + start, DIL_RADIUS), win)
            q, k, v = qr[rows, :], kr[krows, :], vr[krows, :]
            outs, lses = [], []
            for h in range(2):
                qh = jnp.where(head0 if h == 0 else jnp.logical_not(head0), q, jnp.zeros_like(q))
                s = lax.dot_general(qh, k, (((1,), (1,)), ((), ())), preferred_element_type=F32)
                s = s + b_ref[g, var, h]
                m = jnp.max(s, axis=-1, keepdims=True)
                p = jnp.exp(s - m)
                l = jnp.sum(p, axis=-1, keepdims=True)
                outs.append(jnp.dot(p.astype(BF16), v, preferred_element_type=F32) / l)
                lses.append(m + jnp.log(l))
            out_o[rows, :] = jnp.where(head0, outs[0], outs[1])
            out_l[rows, :] = jnp.where(head0, lses[0], lses[1])
            return carry

        lax.fori_loop(0, n_sub * nq, body, 0, unroll=min(unroll, n_sub * nq))

    def regroup(p_ref, dil):
        n_l, L = PERM_BLOCK // dil, S // dil

        def body(c, carry):
            r0 = pl.multiple_of(c * PERM_BLOCK, PERM_BLOCK)
            x = jnp.concatenate([src[0, pl.ds(r0, PERM_BLOCK), :] for src in (q_ref, k_ref, v_ref)], axis=1)
            y = jnp.dot(p_ref[...], x, preferred_element_type=F32).astype(BF16)
            for j, dst in enumerate((qp, kp, vp)):
                for r in range(dil):
                    dst[pl.ds(pl.multiple_of(r * L + c * n_l, n_l), n_l), :] = (
                        y[r * n_l:(r + 1) * n_l, j * LANES:(j + 1) * LANES])
            return carry

        lax.fori_loop(0, S // PERM_BLOCK, body, 0, unroll=4)

    def merge_back(pt_ref, dil, last):
        n_l, L = PERM_BLOCK // dil, S // dil

        def body(c, carry):
            piece = lambda ref: jnp.concatenate(
                [ref[pl.ds(pl.multiple_of(r * L + c * n_l, n_l), n_l), :] for r in range(dil)], axis=0)
            x = piece(lp)
            x1 = x.astype(BF16)
            x2 = (x - x1.astype(F32)).astype(BF16)
            x3 = (x - x1.astype(F32) - x2.astype(F32)).astype(BF16)
            back = jnp.dot(pt_ref[...], jnp.concatenate([piece(op).astype(BF16), x1, x2, x3], axis=1),
                           preferred_element_type=F32)
            o_g = back[:, :LANES]
            lse_g = back[:, LANES:2 * LANES] + back[:, 2 * LANES:3 * LANES] + back[:, 3 * LANES:]
            rows = pl.ds(pl.multiple_of(c * PERM_BLOCK, PERM_BLOCK), PERM_BLOCK)
            lse0 = lsc[rows, :]
            m = jnp.maximum(lse0, lse_g)
            w0, w1 = jnp.exp(lse0 - m), jnp.exp(lse_g - m)
            den = w0 + w1
            o_new = (w0 * osc[rows, :] + w1 * o_g) / den
            if last:
                o_ref[0, rows, :] = o_new.astype(o_ref.dtype)
            else:
                osc[rows, :] = o_new
                lsc[rows, :] = m + jnp.log(den)
            return carry

        lax.fori_loop(0, S // PERM_BLOCK, body, 0, unroll=4)

    attend(q_ref.at[0], k_ref.at[0], v_ref.at[0], 0, S, 1, osc, lsc)
    regroup(p4_ref, 4)
    attend(qp, kp, vp, 1, S // 4, 4, op, lp)
    merge_back(p4t_ref, 4, False)
    regroup(p16_ref, 16)
    attend(qp, kp, vp, 2, S // 16, 16, op, lp)
    merge_back(p16t_ref, 16, True)


DIL_TQ = 128


def dilated_bias_tiles(table):
    return jnp.stack([_dilated_bias(table, d, DIL_TQ) for _, d in DILATED_PATTERNS])


def dilated_attention_fused(zqkv, bias, B, S, unroll=8):
    tq = DIL_TQ
    dils = tuple(d for _, d in DILATED_PATTERNS)
    assert dils == (1, 4, 16) and S % PERM_BLOCK == 0 and (S // 16) % tq == 0 and S // 16 >= tq + 2 * DIL_RADIUS
    zv = zqkv.reshape(B, S, 3 * W_A)
    hb = W_A // LANES
    win = tq + 2 * DIL_RADIUS
    p4, p16 = _residue_perm(4), _residue_perm(16)
    pspec = pl.BlockSpec((PERM_BLOCK, PERM_BLOCK), lambda b, h: (0, 0))
    out = pl.pallas_call(
        functools.partial(_dilated_fused_kernel, S=S, tq=tq, unroll=unroll),
        grid=(B, hb),
        in_specs=[pl.BlockSpec((1, S, LANES), lambda b, h: (b, 0, h)),
                  pl.BlockSpec((1, S, LANES), lambda b, h: (b, 0, hb + h)),
                  pl.BlockSpec((1, S, LANES), lambda b, h: (b, 0, 2 * hb + h)),
                  pl.BlockSpec((3, 3, 2, tq, win), lambda b, h: (0, 0, h, 0, 0)),
                  pspec, pspec, pspec, pspec],
        out_specs=pl.BlockSpec((1, S, LANES), lambda b, h: (b, 0, h)),
        out_shape=jax.ShapeDtypeStruct((B, S, W_A), BF16),
        scratch_shapes=[pltpu.VMEM((S, LANES), BF16)] * 3 + [pltpu.VMEM((S, LANES), F32)] * 4,
        compiler_params=_params(("parallel", "parallel")),
        name="dilated_attn",
    )(zv, zv, zv, bias, p4, p4.T, p16, p16.T)
    return out.reshape(B * S, W_A)


def _lru_kernel(xr_ref, gr_ref, cw_ref, cb_ref, wa_ref, ba_ref, wx_ref, bx_ref, lc_ref, o_ref, hf_ref, *, S, C, unroll):
    nC = S // C
    row = lax.broadcasted_iota(jnp.int32, (C, LANES), 0)
    E = C + 16

    def conv_chunk(c):
        t0 = pl.multiple_of(c * C, C)
        cur = xr_ref[0, pl.ds(t0, C), :]
        pstart = pl.multiple_of(jnp.maximum(t0 - 8, 0), 8)
        nstart = pl.multiple_of(jnp.minimum(t0 + C, S - 8), 8)
        prev8 = jnp.where(c > 0, xr_ref[0, pl.ds(pstart, 8), :], 0.0)
        next8 = jnp.where(c < nC - 1, xr_ref[0, pl.ds(nstart, 8), :], 0.0)
        ext = jnp.concatenate([prev8, cur, next8], axis=0)
        xc = cb_ref[...] + ext[8:8 + C] * cw_ref[1:2, :]
        xc += pltpu.roll(ext, 1, 0)[8:8 + C] * cw_ref[0:1, :]
        xc += pltpu.roll(ext, E - 1, 0)[8:8 + C] * cw_ref[2:3, :]
        xc += pltpu.roll(ext, E - 2, 0)[8:8 + C] * cw_ref[3:4, :]
        return t0, xc

    def gates(xc, d):
        xb = xc.astype(BF16)
        r = jax.nn.sigmoid(jnp.dot(xb, wa_ref[d], preferred_element_type=F32) + ba_ref[d])
        i = jax.nn.sigmoid(jnp.dot(xb, wx_ref[d], preferred_element_type=F32) + bx_ref[d])
        log_a = lc_ref[d] * r
        a = jnp.exp(log_a)
        u = jnp.sqrt(-jnp.tanh(log_a) * (a * a + 1.0)) * (i * xc)
        return a, u

    def scan(a, u, reverse):
        k = 1
        while k < C:
            shift = C - k if reverse else k
            valid = (row < C - k) if reverse else (row >= k)
            u = u + a * jnp.where(valid, pltpu.roll(u, shift, 0), 0.0)
            a = a * jnp.where(valid, pltpu.roll(a, shift, 0), 1.0)
            k *= 2
        return a, u

    def fwd(c, carry):
        t0, xc = conv_chunk(c)
        a, u = scan(*gates(xc, 0), False)
        h = u + a * carry
        hf_ref[pl.ds(t0, C), :] = h
        return h[C - 1:C, :]

    lax.fori_loop(0, nC, fwd, jnp.zeros((1, LANES), F32), unroll=unroll)

    def bwd(j, carry):
        t0, xc = conv_chunk(nC - 1 - j)
        a, u = scan(*gates(xc, 1), True)
        h = u + a * carry
        g = jax.nn.gelu(gr_ref[0, pl.ds(t0, C), :])
        o_ref[0, pl.ds(t0, C), :] = (g * (hf_ref[pl.ds(t0, C), :] + h)).astype(o_ref.dtype)
        return h[0:1, :]

    lax.fori_loop(0, nC, bwd, jnp.zeros((1, LANES), F32), unroll=unroll)


def rg_lru(xg, conv_w, conv_b, wa, ba, wx, bx, lam, B, S, C=128, unroll=4):
    assert S % C == 0
    xv = xg.reshape(B, S, 2 * D_RNN)
    lc = (-LRU_C * jax.nn.softplus(-lam)).reshape(2, LRU_BLOCKS, 1, LRU_BW)
    vec = lambda t: t.reshape(2, LRU_BLOCKS, 1, LRU_BW)
    wspec = pl.BlockSpec((2, None, LRU_BW, LRU_BW), lambda b, n: (0, n, 0, 0))
    vspec = pl.BlockSpec((2, None, 1, LRU_BW), lambda b, n: (0, n, 0, 0))
    out = pl.pallas_call(
        functools.partial(_lru_kernel, S=S, C=C, unroll=unroll),
        grid=(B, LRU_BLOCKS),
        in_specs=[pl.BlockSpec((1, S, LRU_BW), lambda b, n: (b, 0, n)),
                  pl.BlockSpec((1, S, LRU_BW), lambda b, n: (b, 0, LRU_BLOCKS + n)),
                  pl.BlockSpec((CONV_W, LRU_BW), lambda b, n: (0, n)),
                  pl.BlockSpec((1, LRU_BW), lambda b, n: (0, n)),
                  wspec, vspec, wspec, vspec, vspec],
        out_specs=pl.BlockSpec((1, S, LRU_BW), lambda b, n: (b, 0, n)),
        out_shape=jax.ShapeDtypeStruct((B, S, D_RNN), BF16),
        scratch_shapes=[pltpu.VMEM((S, LRU_BW), F32)],
        compiler_params=_params(("parallel", "parallel")),
        name="rg_lru",
    )(xv, xv, conv_w, conv_b.reshape(1, D_RNN), wa.astype(BF16), vec(ba), wx.astype(BF16), vec(bx), lc)
    return out.reshape(B * S, D_RNN)


BIAS_SATURATION = MAX_DISTANCE // 2 + 1


def _diff_bias(table, tq, tk, e_lo, e_hi):
    rel = (jnp.arange(e_lo, e_hi + 1, dtype=jnp.int32)[:, None] * tk - (tq - 1)
           + jnp.arange(tq + tk, dtype=jnp.int32)[None, :])
    return _toeplitz(table[_t5_bucket(rel)].astype(F32).transpose(2, 0, 1), tq, tk)


def _lane_tile(x, n):
    return x if n == 1 else jnp.concatenate([x] * n, axis=1)


def _diff_kernel(lam_ref, cb_ref, q_ref, k_ref, v_ref, b_ref, g_ref, o_ref, m_sc, l_sc, acc_sc, *, nk, e_lo, e_hi, mult, rs,
                 unroll):
    kv = pl.program_id(3)
    t = q_ref.shape[1]
    tk = k_ref.shape[1]
    e = kv - (t // tk) * pl.program_id(2)

    @pl.when(kv == 0)
    def _():
        m_sc[...] = jnp.full(m_sc.shape, -jnp.inf, F32)
        l_sc[...] = jnp.zeros(l_sc.shape, F32)
        acc_sc[...] = jnp.zeros(acc_sc.shape, F32)

    def sub(r, carry, *, const_bias):
        r0 = pl.multiple_of(r * rs, rs)
        rows = pl.ds(r0, rs)
        q = q_ref[0, rows, :]
        bias = cb_ref[pl.program_id(1), (e > 0).astype(jnp.int32)] if const_bias else b_ref[0, 0, rows, :]
        for mi in range(2):
            sl = slice(mi * DK_C, (mi + 1) * DK_C)
            s = lax.dot_general(q[:, sl], k_ref[0, :, sl], (((1,), (1,)), ((), ())), preferred_element_type=F32)
            if not const_bias:
                s = s + bias
            m_cur = jnp.max(s, axis=-1, keepdims=True)
            m_prev = m_sc[mi, rows, :]
            m_new = jnp.maximum(m_prev, m_cur + bias if const_bias else m_cur)
            alpha = jnp.exp2(m_prev - m_new)
            p = jnp.exp2(s - _lane_tile(m_new - bias if const_bias else m_new, tk // LANES))
            l_sc[mi, rows, :] = alpha * l_sc[mi, rows, :] + jnp.sum(p, axis=-1, keepdims=True)
            acc_sc[mi, rows, :] = (_lane_tile(alpha, DV_C // LANES) * acc_sc[mi, rows, :]
                                   + jnp.dot(p.astype(BF16), v_ref[0], preferred_element_type=F32))
            m_sc[mi, rows, :] = m_new
        return carry

    far = jnp.logical_or(e >= e_hi, e <= e_lo)
    for const_bias in (False, True):
        @pl.when(far == const_bias)
        def _():
            lax.fori_loop(0, t // rs, functools.partial(sub, const_bias=const_bias), 0,
                          unroll=min(unroll, t // rs))

    @pl.when(kv == nk - 1)
    def _():
        rep = lambda x: _lane_tile(x, DV_C // LANES)
        o = acc_sc[0] / rep(l_sc[0]) - lam_ref[0, 0] * (acc_sc[1] / rep(l_sc[1]))
        o_ref[0] = (_rms(o, g_ref[...]) * mult).astype(o_ref.dtype)


def diff_bias_tiles(table, tq=2048, tk=512):
    e_hi = -(-(BIAS_SATURATION + tq - 1) // tk)
    e_lo = -(-(-(BIAS_SATURATION + tk - 1) // tk))
    return _diff_bias(table, tq, tk, e_lo, e_hi) * LOG2E, e_lo, e_hi


def diff_attention(z, bias_tiles, lam, subln, lambda_init, B, S, rs=128, unroll=16):
    bias, e_lo, e_hi = bias_tiles
    tq, tk = bias.shape[2:]
    assert S % tq == 0 and S % tk == 0 and tq % tk == 0 and tq % rs == 0
    u = tq // tk
    zv = z.reshape(B, S, z.shape[1])
    blk = 2 * DK_C
    assert blk == DV_C
    kb, vb = H_C, 2 * H_C
    out = pl.pallas_call(
        functools.partial(_diff_kernel, nk=S // tk, e_lo=e_lo, e_hi=e_hi, mult=1.0 - lambda_init, rs=rs, unroll=unroll),
        grid=(B, H_C, S // tq, S // tk),
        in_specs=[pl.BlockSpec(memory_space=pltpu.SMEM),
                  pl.BlockSpec(memory_space=pltpu.SMEM),
                  pl.BlockSpec((1, tq, blk), lambda b, h, i, j: (b, i, h)),
                  pl.BlockSpec((1, tk, blk), lambda b, h, i, j: (b, j, kb + h)),
                  pl.BlockSpec((1, tk, DV_C), lambda b, h, i, j: (b, j, vb + h)),
                  pl.BlockSpec((1, 1, tq, tk), lambda b, h, i, j: (h, jnp.clip(j - u * i, e_lo, e_hi) - e_lo, 0, 0)),
                  pl.BlockSpec((1, DV_C), lambda b, h, i, j: (0, 0))],
        out_specs=pl.BlockSpec((1, tq, DV_C), lambda b, h, i, j: (b, i, h)),
        out_shape=jax.ShapeDtypeStruct((B, S, H_C * DV_C), BF16),
        scratch_shapes=[pltpu.VMEM((2, tq, LANES), F32), pltpu.VMEM((2, tq, LANES), F32),
                        pltpu.VMEM((2, tq, DV_C), F32)],
        compiler_params=_params(("parallel", "parallel", "parallel", "arbitrary")),
        name="diff_attn",
    )(lam.reshape(1, 1), bias[:, (0, e_hi - e_lo), 0, 0], zv, zv, zv, bias, subln.reshape(1, DV_C))
    return out.reshape(B * S, H_C * DV_C)


def _matmul_res_kernel(a_ref, w_ref, x_ref, o_ref):
    o_ref[...] = x_ref[...] + jnp.dot(a_ref[...], w_ref[...], preferred_element_type=F32)


def matmul_residual(a, w, x, tm):
    T, D = x.shape
    K = a.shape[1]
    tm = min(tm, T)
    assert T % tm == 0
    return pl.pallas_call(
        _matmul_res_kernel,
        grid=(T // tm,),
        in_specs=[pl.BlockSpec((tm, K), lambda i: (i, 0)),
                  pl.BlockSpec((K, D), lambda i: (0, 0)),
                  pl.BlockSpec((tm, D), lambda i: (i, 0))],
        out_specs=pl.BlockSpec((tm, D), lambda i: (i, 0)),
        out_shape=jax.ShapeDtypeStruct((T, D), F32),
        compiler_params=_params(("parallel",)),
        name="matmul_residual",
    )(a, w, x)


def _cross_kernel(x_ref, g_ref, wq_ref, kv_ref, wo_ref, o_ref):
    x = x_ref[0]
    h = _rms(x, g_ref[...]).astype(BF16)
    q = jnp.dot(h, wq_ref[...], preferred_element_type=F32).astype(BF16)
    kv = kv_ref[0]
    hw = H_X * DH_X
    outs = []
    for hd in range(H_X):
        sl = slice(hd * DH_X, (hd + 1) * DH_X)
        s = lax.dot_general(q[:, sl], kv[:, sl], (((1,), (1,)), ((), ())), preferred_element_type=F32)
        m = jnp.max(s, axis=-1, keepdims=True)
        p = jnp.exp(s - m)
        l = jnp.sum(p, axis=-1, keepdims=True)
        vh = kv[:, hw + hd * DH_X: hw + (hd + 1) * DH_X]
        outs.append((jnp.dot(p.astype(BF16), vh, preferred_element_type=F32) / l).astype(BF16))
    o = jnp.concatenate(outs, axis=-1)
    o_ref[0] = x + jnp.dot(o, wo_ref[...], preferred_element_type=F32)


def cross_attention(x, g, wq, kvn, wo, B, S, tm):
    D = x.shape[1]
    tm = min(tm, S)
    assert S % tm == 0
    hw = H_X * DH_X
    out = pl.pallas_call(
        _cross_kernel,
        grid=(B, S // tm),
        in_specs=[pl.BlockSpec((1, tm, D), lambda b, i: (b, i, 0)),
                  pl.BlockSpec((1, D), lambda b, i: (0, 0)),
                  pl.BlockSpec((D, hw), lambda b, i: (0, 0)),
                  pl.BlockSpec((1, N_MEM, 2 * hw), lambda b, i: (b, 0, 0)),
                  pl.BlockSpec((hw, D), lambda b, i: (0, 0))],
        out_specs=pl.BlockSpec((1, tm, D), lambda b, i: (b, i, 0)),
        out_shape=jax.ShapeDtypeStruct((B, S, D), F32),
        compiler_params=_params(("parallel", "parallel")),
        name="cross_attn",
    )(x.reshape(B, S, D), g.reshape(1, D), wq, kvn.reshape(B, N_MEM, 2 * hw), wo)
    return out.reshape(B * S, D)


ROUTE_BIG = 1 << 20


def _router_kernel(x_ref, g_ref, w_ref, b_ref, o_ref):
    h = _rms(x_ref[...], g_ref[...])
    h_hi = h.astype(BF16)
    h_lo = (h - h_hi.astype(F32)).astype(BF16)
    a = jnp.dot(h_hi, w_ref[...], preferred_element_type=F32)
    lg = (a[:, :LANES] + a[:, LANES:] + jnp.dot(h_lo, w_ref[:, :LANES], preferred_element_type=F32)) + b_ref[...]
    lane = lax.broadcasted_iota(jnp.int32, lg.shape, 1)
    first = lambda mask: jnp.min(jnp.where(mask, lane, ROUTE_BIG), axis=-1, keepdims=True)
    is_g = lane < N_GROUPS
    gl = jnp.where(is_g, lg, -jnp.inf)
    gmax = jnp.max(gl, axis=-1, keepdims=True)
    gidx = first(gl == gmax)
    pg = 1.0 / jnp.sum(jnp.where(is_g, jnp.exp(lg - gmax), 0.0), axis=-1, keepdims=True)
    e_id = lane - N_GROUPS
    in_grp = (e_id >= 0) & (e_id < N_EXPERTS) & ((e_id // EXPERTS_PER_GROUP) == gidx)
    el = jnp.where(in_grp, lg, -jnp.inf)
    v0 = jnp.max(el, axis=-1, keepdims=True)
    i0 = first(el == v0)
    el2 = jnp.where(lane == i0, -jnp.inf, el)
    v1 = jnp.max(el2, axis=-1, keepdims=True)
    i1 = first(el2 == v1)
    t = jnp.exp(v1 - v0)
    w0 = pg / (1.0 + t)
    w1 = w0 * t
    res = jnp.where(lane == 0, (i0 - N_GROUPS).astype(F32),
                    jnp.where(lane == 1, (i1 - N_GROUPS).astype(F32),
                              jnp.where(lane == 2, w0, jnp.where(lane == 3, w1, 0.0))))
    o_ref[...] = res


def moe_router(x, g, wg, bg, we, be, tm):
    T, D = x.shape
    tm = min(tm, T)
    wr = jnp.zeros((D, LANES), F32).at[:, :N_GROUPS].set(wg).at[:, N_GROUPS:N_GROUPS + N_EXPERTS].set(we)
    br = jnp.zeros((1, LANES), F32).at[0, :N_GROUPS].set(bg).at[0, N_GROUPS:N_GROUPS + N_EXPERTS].set(be)
    w_hi = wr.astype(BF16)
    wr = jnp.concatenate([w_hi, (wr - w_hi.astype(F32)).astype(BF16)], axis=1)
    return pl.pallas_call(
        _router_kernel,
        grid=(T // tm,),
        in_specs=[pl.BlockSpec((tm, D), lambda i: (i, 0)),
                  pl.BlockSpec((1, D), lambda i: (0, 0)),
                  pl.BlockSpec((D, 2 * LANES), lambda i: (0, 0)),
                  pl.BlockSpec((1, LANES), lambda i: (0, 0))],
        out_specs=pl.BlockSpec((tm, LANES), lambda i: (i, 0)),
        out_shape=jax.ShapeDtypeStruct((T, LANES), F32),
        compiler_params=_params(("parallel",)),
        name="moe_router",
    )(x, g.reshape(1, D), wr, br)


GATHER_UNROLL = 8


def _start_row_gather(idx_ref, src_hbm, dst, sem, n):
    def body(r, c):
        pltpu.make_async_copy(src_hbm.at[pl.ds(idx_ref[0, 0, r], 1)], dst.at[pl.ds(r, 1)], sem).start()
        return c

    lax.fori_loop(0, n, body, 0, unroll=GATHER_UNROLL)


def _start_row_gather_unrolled(idx_ref, src_hbm, dst, sem, n):
    for r in range(n):
        pltpu.make_async_copy(src_hbm.at[pl.ds(idx_ref[0, 0, r], 1)], dst.at[pl.ds(r, 1)], sem).start()


def _wait_row_gather(src_hbm, dst, sem, n):
    def body(r, c):
        pltpu.make_async_copy(src_hbm.at[pl.ds(0, 1)], dst.at[pl.ds(0, 1)], sem).wait()
        return c

    lax.fori_loop(0, n, body, 0, unroll=GATHER_UNROLL)


def _expert_kernel(be_ref, nu_ref, tok_ref, tok_next_ref, x_hbm, g_ref, wg_ref, wu_ref, wd_ref, o_ref, xbuf, sem, *, blk):
    i = pl.program_id(0)
    nu = nu_ref[0]
    slot = i % 2

    @pl.when(jnp.logical_and(i == 0, nu > 0))
    def _():
        _start_row_gather(tok_ref, x_hbm, xbuf.at[0], sem.at[0], blk)

    @pl.when(i < nu)
    def _():
        _wait_row_gather(x_hbm, xbuf.at[slot], sem.at[slot], blk)
        _start_row_gather_unrolled(tok_next_ref, x_hbm, xbuf.at[1 - slot], sem.at[1 - slot], blk)
        h = _rms(xbuf[slot], g_ref[...]).astype(BF16)
        a = jnp.dot(h, wg_ref[0], preferred_element_type=F32)
        b = jnp.dot(h, wu_ref[0], preferred_element_type=F32)
        hid = (jax.nn.silu(a) * b).astype(BF16)
        o_ref[...] = jnp.dot(hid, wd_ref[0], preferred_element_type=F32)

    @pl.when(i == nu - 1)
    def _():
        _wait_row_gather(x_hbm, xbuf.at[1 - slot], sem.at[1 - slot], blk)

    @pl.when(i >= nu)
    def _():
        o_ref[...] = jnp.zeros(o_ref.shape, o_ref.dtype)


def moe_experts(x, g, buf_tok, blk_e, n_used, w_gate, w_up, w_down, blk):
    T, D = x.shape
    cap = buf_tok.shape[0]
    n_blk = cap // blk
    tok = buf_tok.reshape(n_blk, 1, blk)
    grid_spec = pltpu.PrefetchScalarGridSpec(
        num_scalar_prefetch=2,
        grid=(n_blk,),
        in_specs=[pl.BlockSpec((1, 1, blk), lambda i, be, nu: (i, 0, 0), memory_space=pltpu.SMEM),
                  pl.BlockSpec((1, 1, blk), lambda i, be, nu: (jnp.minimum(i + 1, jnp.maximum(nu[0] - 1, 0)), 0, 0),
                               memory_space=pltpu.SMEM),
                  pl.BlockSpec(memory_space=pl.ANY),
                  pl.BlockSpec((1, D), lambda i, be, nu: (0, 0)),
                  pl.BlockSpec((1, D, D_FF), lambda i, be, nu: (be[i], 0, 0)),
                  pl.BlockSpec((1, D, D_FF), lambda i, be, nu: (be[i], 0, 0)),
                  pl.BlockSpec((1, D_FF, D), lambda i, be, nu: (be[i], 0, 0))],
        out_specs=pl.BlockSpec((blk, D), lambda i, be, nu: (i, 0)),
        scratch_shapes=[pltpu.VMEM((2, blk, D), F32), pltpu.SemaphoreType.DMA((2,))],
    )
    return pl.pallas_call(
        functools.partial(_expert_kernel, blk=blk),
        grid_spec=grid_spec,
        out_shape=jax.ShapeDtypeStruct((cap, D), F32),
        compiler_params=_params(("arbitrary",)),
        name="moe_experts",
    )(blk_e, n_used, tok, tok, x, g.reshape(1, D), w_gate, w_up, w_down)


def _combine_kernel(d_ref, d_next_ref, x_ref, r_ref, y_hbm, g_ref, o_ref, buf, sem, *, tt, final):
    i = pl.program_id(0)
    slot = i % 2

    @pl.when(i == 0)
    def _():
        _start_row_gather(d_ref, y_hbm, buf.at[0], sem.at[0], 2 * tt)

    _wait_row_gather(y_hbm, buf.at[slot], sem.at[slot], 2 * tt)
    _start_row_gather_unrolled(d_next_ref, y_hbm, buf.at[1 - slot], sem.at[1 - slot], 2 * tt)
    r = r_ref[...]
    y = x_ref[...] + r[:, 2:3] * buf[slot, 0:tt, :] + r[:, 3:4] * buf[slot, tt:2 * tt, :]
    o_ref[...] = _rms(y, g_ref[...]) if final else y

    @pl.when(i == pl.num_programs(0) - 1)
    def _():
        _wait_row_gather(y_hbm, buf.at[1 - slot], sem.at[1 - slot], 2 * tt)


def moe_combine(x, routed, dest, y, g_final, tt):
    T, D = x.shape
    n = T // tt
    d = dest.reshape(n, tt, 2).transpose(0, 2, 1).reshape(n, 1, 2 * tt)
    final = g_final is not None
    g = (g_final if final else jnp.ones((D,), F32)).reshape(1, D)
    return pl.pallas_call(
        functools.partial(_combine_kernel, tt=tt, final=final),
        grid=(n,),
        in_specs=[pl.BlockSpec((1, 1, 2 * tt), lambda i: (i, 0, 0), memory_space=pltpu.SMEM),
                  pl.BlockSpec((1, 1, 2 * tt), lambda i: (jnp.minimum(i + 1, n - 1), 0, 0), memory_space=pltpu.SMEM),
                  pl.BlockSpec((tt, D), lambda i: (i, 0)),
                  pl.BlockSpec((tt, LANES), lambda i: (i, 0)),
                  pl.BlockSpec(memory_space=pl.ANY),
                  pl.BlockSpec((1, D), lambda i: (0, 0))],
        out_specs=pl.BlockSpec((tt, D), lambda i: (i, 0)),
        out_shape=jax.ShapeDtypeStruct((T, D), F32),
        scratch_shapes=[pltpu.VMEM((2, 2 * tt, D), F32), pltpu.SemaphoreType.DMA((2,))],
        compiler_params=_params(("arbitrary",)),
        name="moe_combine",
    )(d, d, x, routed, y, g)


def _route_plan(routed, blk):
    T = routed.shape[0]
    e = routed[:, :2].astype(jnp.int32)
    flat_e = e.reshape(-1)
    n_rows = 2 * T
    onehot = (flat_e[:, None] == jnp.arange(N_EXPERTS, dtype=jnp.int32)[None, :]).astype(jnp.int32)
    cum = jnp.cumsum(onehot, axis=0)
    rank = jnp.sum(cum * onehot, axis=1) - 1
    counts = cum[-1]
    padded = (counts + blk - 1) // blk * blk
    pad_end = jnp.cumsum(padded)
    pad_start = pad_end - padded
    dest = pad_start[flat_e] + rank
    n_blk = -(-n_rows // blk) + N_EXPERTS
    cap = n_blk * blk
    flat_t = jnp.repeat(jnp.arange(T, dtype=jnp.int32), 2)
    buf_tok = jnp.zeros((cap,), jnp.int32).at[dest].set(flat_t)
    blk_e = jnp.minimum(jnp.searchsorted(pad_end, jnp.arange(n_blk, dtype=jnp.int32) * blk, side='right'),
                        N_EXPERTS - 1).astype(jnp.int32)
    n_used = (pad_end[-1:] // blk).astype(jnp.int32)
    return buf_tok, dest.reshape(T, 2), blk_e, n_used


def hier_moe(x, g, wg, bg, we, be, w_gate, w_up, w_down, g_final=None, blk=256, tm=512, tt=256):
    routed = moe_router(x, g, wg, bg, we, be, tm)
    buf_tok, dest, blk_e, n_used = _route_plan(routed, blk)
    y = moe_experts(x, g, buf_tok, blk_e, n_used, w_gate, w_up, w_down, blk)
    return moe_combine(x, routed, dest, y, g_final, min(tt, x.shape[0]))


def _lambda_init(layer):
    return 0.8 - 0.6 * math.exp(-0.3 * layer)


def _prep_weights(p):
    w = dict(p)
    col = lambda n: jnp.arange(n, dtype=jnp.int32)
    ab = p["ab_w_in"]
    w["ab_w_in"] = (ab * jnp.where(col(ab.shape[-1]) < W_A, DH_A ** -0.5, 1.0)).astype(BF16)
    w["ab_w_out"] = p["ab_w_out"].astype(BF16)
    cw = p["c_w_in"]
    w["c_w_in"] = (cw * jnp.where(col(cw.shape[-1]) < H_C * 2 * DK_C, DK_C ** -0.5 * LOG2E, 1.0)).astype(BF16)
    w["c_w_out"] = p["c_w_out"].astype(BF16)
    w["x_wq"] = (p["x_wq"] * DH_X ** -0.5).astype(BF16)
    w["x_wkv"] = p["x_wkv"].astype(BF16)
    w["x_wo"] = p["x_wo"].astype(BF16)
    for n in ("moe_w_gate", "moe_w_up", "moe_w_down"):
        w[n] = p[n].astype(BF16)
    lv = p["c_lam"].astype(F32)
    w["c_lam_scalar"] = [jnp.exp(jnp.sum(lv[i, 0] * lv[i, 1])) - jnp.exp(jnp.sum(lv[i, 2] * lv[i, 3]))
                         + _lambda_init(2 * i + 1) for i in range(lv.shape[0])]
    w["dil_bias"] = dilated_bias_tiles(p["rel_bias"][:, :H_A])
    w["diff_bias"] = diff_bias_tiles(p["rel_bias"][:, H_A:])
    return w


def _trunk(x3, mem3, w):
    B, S, D = x3.shape
    x = x3.reshape(B * S, D)
    mem = mem3.reshape(B * N_MEM, D)
    for l in range(DEPTH):
        i = l // 2
        if l % 2 == 0:
            zqkv = norm_matmul(x, w["norm_mix"][l], w["ab_w_in"][i], BF16, 1024, 1024, 0, 3 * W_A)
            xg = norm_matmul(x, w["norm_mix"][l], w["ab_w_in"][i], F32, 1024, 1024, 3 * W_A, 2 * D_RNN)
            attn = dilated_attention_fused(zqkv, w["dil_bias"], B, S)
            rec = rg_lru(xg, w["ab_conv_w"][i], w["ab_conv_b"][i], w["lru_wa"][i], w["lru_ba"][i],
                         w["lru_wx"][i], w["lru_bx"][i], w["lru_lam"][i], B, S)
            x = mix_out(attn, rec, w["ab_w_out"][i], x, 512)
        else:
            z = norm_matmul(x, w["norm_mix"][l], w["c_w_in"][i], BF16, 1024, 1024)
            o = diff_attention(z, w["diff_bias"], w["c_lam_scalar"][i], w["c_subln"][i], _lambda_init(l), B, S)
            x = matmul_residual(o, w["c_w_out"][i], x, 512)
        kvn = norm_matmul(mem, w["norm_mem"][l], w["x_wkv"][l], BF16, 1024, 1024)
        x = cross_attention(x, w["norm_cross"][l], w["x_wq"][l], kvn, w["x_wo"][l], B, S, 512)
        x = hier_moe(x, w["norm_ffn"][l], w["moe_wg"][l], w["moe_bg"][l], w["moe_we"][l], w["moe_be"][l],
                     w["moe_w_gate"][l], w["moe_w_up"][l], w["moe_w_down"][l],
                     g_final=w["norm_final"] if l == DEPTH - 1 else None)
    return x.reshape(B, S, D)


def kernel(x_prompt, x_sample, mem_prompt, mem_sample, rel_bias, ab_w_in, ab_conv_w, ab_conv_b, lru_wa, lru_ba, lru_wx, lru_bx, lru_lam, ab_w_out, c_w_in, c_lam, c_subln, c_w_out, norm_mix, norm_cross, norm_mem, x_wq, x_wkv, x_wo, norm_ffn, moe_wg, moe_bg, moe_we, moe_be, moe_w_gate, moe_w_up, moe_w_down, norm_final):
    w = _prep_weights(dict(
        rel_bias=rel_bias, ab_w_in=ab_w_in, ab_conv_w=ab_conv_w, ab_conv_b=ab_conv_b, lru_wa=lru_wa, lru_ba=lru_ba,
        lru_wx=lru_wx, lru_bx=lru_bx, lru_lam=lru_lam, ab_w_out=ab_w_out, c_w_in=c_w_in, c_lam=c_lam,
        c_subln=c_subln, c_w_out=c_w_out, norm_mix=norm_mix, norm_cross=norm_cross, norm_mem=norm_mem, x_wq=x_wq,
        x_wkv=x_wkv, x_wo=x_wo, norm_ffn=norm_ffn, moe_wg=moe_wg, moe_bg=moe_bg, moe_we=moe_we, moe_be=moe_be,
        moe_w_gate=moe_w_gate, moe_w_up=moe_w_up, moe_w_down=moe_w_down, norm_final=norm_final))
    return _trunk(x_prompt, mem_prompt, w), _trunk(x_sample, mem_sample, w)
```

```python
import functools
import math

import jax
import jax.numpy as jnp
from jax import lax
from jax.experimental import pallas as pl
from jax.experimental.pallas import tpu as pltpu

F32 = jnp.float32
BF16 = jnp.bfloat16

D_MODEL = 2048
DEPTH = 4
H_A, DH_A = 16, 64
W_A = H_A * DH_A
DILATED_PATTERNS = ((128, 1), (512, 4), (2048, 16))
D_RNN = 1024
LRU_BLOCKS, LRU_BW = 8, 128
CONV_W = 4
LRU_C = 8.0
H_C, DK_C, DV_C = 8, 128, 256
NUM_BUCKETS, MAX_DISTANCE = 32, 2048
H_X, DH_X, N_MEM = 4, 128, 256
N_GROUPS, EXPERTS_PER_GROUP = 4, 4
N_EXPERTS = N_GROUPS * EXPERTS_PER_GROUP
D_FF = 1024
EPS = 1e-6
NEG = -1e30
LOG2E = math.log2(math.e)

LANES = 128
VMEM_LIMIT = 56 * 1024 * 1024


def _params(sem):
    return pltpu.CompilerParams(dimension_semantics=sem, vmem_limit_bytes=VMEM_LIMIT)


def _rms(x, g):
    return x * lax.rsqrt(jnp.mean(x * x, axis=-1, keepdims=True) + EPS) * g


def _norm_matmul_kernel(x_ref, g_ref, w_ref, o_ref, h_ref):
    @pl.when(pl.program_id(1) == 0)
    def _():
        h_ref[...] = _rms(x_ref[...], g_ref[...]).astype(BF16)

    o_ref[...] = jnp.dot(h_ref[...], w_ref[...], preferred_element_type=F32).astype(o_ref.dtype)


def norm_matmul(x, g, w, out_dtype, tm, tn, col0=0, n_cols=None):
    T, D = x.shape
    N = w.shape[1] - col0 if n_cols is None else n_cols
    tm, tn = min(tm, T), min(tn, N)
    assert T % tm == 0 and N % tn == 0 and col0 % tn == 0
    j0 = col0 // tn
    return pl.pallas_call(
        _norm_matmul_kernel,
        grid=(T // tm, N // tn),
        in_specs=[pl.BlockSpec((tm, D), lambda i, j: (i, 0)),
                  pl.BlockSpec((1, D), lambda i, j: (0, 0)),
                  pl.BlockSpec((D, tn), lambda i, j: (0, j0 + j))],
        out_specs=pl.BlockSpec((tm, tn), lambda i, j: (i, j)),
        out_shape=jax.ShapeDtypeStruct((T, N), out_dtype),
        scratch_shapes=[pltpu.VMEM((tm, D), BF16)],
        compiler_params=_params(("parallel", "arbitrary")),
        name="norm_matmul",
    )(x, g.reshape(1, D), w)


def _mix_out_kernel(a1_ref, a2_ref, w_ref, x_ref, o_ref):
    k1 = a1_ref.shape[1]
    acc = jnp.dot(a1_ref[...], w_ref[0:k1, :], preferred_element_type=F32)
    acc += jnp.dot(a2_ref[...], w_ref[k1:, :], preferred_element_type=F32)
    o_ref[...] = x_ref[...] + acc


def mix_out(a1, a2, w, x, tm):
    T, D = x.shape
    K1, K2 = a1.shape[1], a2.shape[1]
    tm = min(tm, T)
    assert T % tm == 0
    return pl.pallas_call(
        _mix_out_kernel,
        grid=(T // tm,),
        in_specs=[pl.BlockSpec((tm, K1), lambda i: (i, 0)),
                  pl.BlockSpec((tm, K2), lambda i: (i, 0)),
                  pl.BlockSpec((K1 + K2, D), lambda i: (0, 0)),
                  pl.BlockSpec((tm, D), lambda i: (i, 0))],
        out_specs=pl.BlockSpec((tm, D), lambda i: (i, 0)),
        out_shape=jax.ShapeDtypeStruct((T, D), F32),
        compiler_params=_params(("parallel",)),
        name="mix_out",
    )(a1, a2, w, x)


def _t5_bucket(rel):
    half = NUM_BUCKETS // 2
    exact = half // 2
    n = jnp.abs(rel)
    large = exact + (jnp.log(jnp.maximum(n, 1).astype(F32) / exact)
                     / math.log(MAX_DISTANCE / exact) * (half - exact)).astype(jnp.int32)
    large = jnp.minimum(large, half - 1)
    return (rel > 0).astype(jnp.int32) * half + jnp.where(n < exact, n, large)


DIL_RADIUS = 64


def _dilated_bias(table, dil, tq):
    win = tq + 2 * DIL_RADIUS
    offs = jnp.array((0, -DIL_RADIUS, -2 * DIL_RADIUS), jnp.int32)
    rel = offs[:, None] - (tq - 1) + jnp.arange(tq + win, dtype=jnp.int32)[None, :]
    w = jnp.where((jnp.abs(rel) <= DIL_RADIUS)[..., None], table[_t5_bucket(rel * dil)].astype(F32), NEG)
    return _toeplitz(w.transpose(0, 2, 1), tq, win)


def _toeplitz_kernel(w_ref, o_ref):
    tq, tk = o_ref.shape[2], o_ref.shape[3]
    x = jnp.broadcast_to(w_ref[0, 0], (tq, w_ref.shape[3]))
    o_ref[0, 0] = pltpu.roll(x, 0, 1, stride=1, stride_axis=0)[:, :tk]


def _toeplitz(w, tq, tk):
    a, b, wd = w.shape
    assert wd == tq + tk and wd % LANES == 0
    w = jnp.roll(w, -(tq - 1), axis=-1).reshape(a, b, 1, wd)
    return pl.pallas_call(
        _toeplitz_kernel,
        grid=(a, b),
        in_specs=[pl.BlockSpec((1, 1, 1, wd), lambda i, j: (i, j, 0, 0))],
        out_specs=pl.BlockSpec((1, 1, tq, tk), lambda i, j: (i, j, 0, 0)),
        out_shape=jax.ShapeDtypeStruct((a, b, tq, tk), F32),
        compiler_params=_params(("parallel", "parallel")),
        name="toeplitz_bias",
    )(w)


PERM_BLOCK = 256


def _residue_perm(dil):
    n_l = PERM_BLOCK // dil
    i = jnp.arange(PERM_BLOCK, dtype=jnp.int32)
    src = (i % n_l) * dil + i // n_l
    return (src[:, None] == i[None, :]).astype(BF16)


def _dilated_fused_kernel(q_ref, k_ref, v_ref, b_ref, p4_ref, p4t_ref, p16_ref, p16t_ref, o_ref,
                          qp, kp, vp, op, lp, osc, lsc, *, S, tq, unroll):
    win = tq + 2 * DIL_RADIUS
    head0 = lax.broadcasted_iota(jnp.int32, (1, LANES), 1) < DH_A

    def attend(qr, kr, vr, g, L, n_sub, out_o, out_l):
        nq = L // tq

        def body(n, carry):
            sub = n // nq
            i = n - sub * nq
            base = sub * L
            i0 = i * tq
            start = jnp.clip(i0 - DIL_RADIUS, 0, L - win)
            var = jnp.where(i == 0, 0, jnp.where(i == nq - 1, 2, 1))
            rows = pl.ds(pl.multiple_of(base + i0, tq), tq)
            krows = pl.ds(pl.multiple_of(base + start, DIL_RADIUS), win)
            q, k, v = qr[rows, :], kr[krows, :], vr[krows, :]
            outs, lses = [], []
            for h in range(2):
                qh = jnp.where(head0 if h == 0 else jnp.logical_not(head0), q, jnp.zeros_like(q))
                s = lax.dot_general(qh, k, (((1,), (1,)), ((), ())), preferred_element_type=F32)
                s = s + b_ref[g, var, h]
                m = jnp.max(s, axis=-1, keepdims=True)
                p = jnp.exp(s - m)
                l = jnp.sum(p, axis=-1, keepdims=True)
                outs.append(jnp.dot(p.astype(BF16), v, preferred_element_type=F32) / l)
                lses.append(m + jnp.log(l))
            out_o[rows, :] = jnp.where(head0, outs[0], outs[1])
            out_l[rows, :] = jnp.where(head0, lses[0], lses[1])
            return carry

        lax.fori_loop(0, n_sub * nq, body, 0, unroll=min(unroll, n_sub * nq))

    def regroup(p_ref, dil):
        n_l, L = PERM_BLOCK // dil, S // dil

        def body(c, carry):
            r0 = pl.multiple_of(c * PERM_BLOCK, PERM_BLOCK)
            x = jnp.concatenate([src[0, pl.ds(r0, PERM_BLOCK), :] for src in (q_ref, k_ref, v_ref)], axis=1)
            y = jnp.dot(p_ref[...], x, preferred_element_type=F32).astype(BF16)
            for j, dst in enumerate((qp, kp, vp)):
                for r in range(dil):
                    dst[pl.ds(pl.multiple_of(r * L + c * n_l, n_l), n_l), :] = (
                        y[r * n_l:(r + 1) * n_l, j * LANES:(j + 1) * LANES])
            return carry

        lax.fori_loop(0, S // PERM_BLOCK, body, 0, unroll=4)

    def merge_back(pt_ref, dil, last):
        n_l, L = PERM_BLOCK // dil, S // dil

        def body(c, carry):
            piece = lambda ref: jnp.concatenate(
                [ref[pl.ds(pl.multiple_of(r * L + c * n_l, n_l), n_l), :] for r in range(dil)], axis=0)
            x = piece(lp)
            x1 = x.astype(BF16)
            x2 = (x - x1.astype(F32)).astype(BF16)
            x3 = (x - x1.astype(F32) - x2.astype(F32)).astype(BF16)
            back = jnp.dot(pt_ref[...], jnp.concatenate([piece(op).astype(BF16), x1, x2, x3], axis=1),
                           preferred_element_type=F32)
            o_g = back[:, :LANES]
            lse_g = back[:, LANES:2 * LANES] + back[:, 2 * LANES:3 * LANES] + back[:, 3 * LANES:]
            rows = pl.ds(pl.multiple_of(c * PERM_BLOCK, PERM_BLOCK), PERM_BLOCK)
            lse0 = lsc[rows, :]
            m = jnp.maximum(lse0, lse_g)
            w0, w1 = jnp.exp(lse0 - m), jnp.exp(lse_g - m)
            den = w0 + w1
            o_new = (w0 * osc[rows, :] + w1 * o_g) / den
            if last:
                o_ref[0, rows, :] = o_new.astype(o_ref.dtype)
            else:
                osc[rows, :] = o_new
                lsc[rows, :] = m + jnp.log(den)
            return carry

        lax.fori_loop(0, S // PERM_BLOCK, body, 0, unroll=4)

    attend(q_ref.at[0], k_ref.at[0], v_ref.at[0], 0, S, 1, osc, lsc)
    regroup(p4_ref, 4)
    attend(qp, kp, vp, 1, S // 4, 4, op, lp)
    merge_back(p4t_ref, 4, False)
    regroup(p16_ref, 16)
    attend(qp, kp, vp, 2, S // 16, 16, op, lp)
    merge_back(p16t_ref, 16, True)


DIL_TQ = 128


def dilated_bias_tiles(table):
    return jnp.stack([_dilated_bias(table, d, DIL_TQ) for _, d in DILATED_PATTERNS])


def dilated_attention_fused(zqkv, bias, B, S, unroll=8):
    tq = DIL_TQ
    dils = tuple(d for _, d in DILATED_PATTERNS)
    assert dils == (1, 4, 16) and S % PERM_BLOCK == 0 and (S // 16) % tq == 0 and S // 16 >= tq + 2 * DIL_RADIUS
    zv = zqkv.reshape(B, S, 3 * W_A)
    hb = W_A // LANES
    win = tq + 2 * DIL_RADIUS
    p4, p16 = _residue_perm(4), _residue_perm(16)
    pspec = pl.BlockSpec((PERM_BLOCK, PERM_BLOCK), lambda b, h: (0, 0))
    out = pl.pallas_call(
        functools.partial(_dilated_fused_kernel, S=S, tq=tq, unroll=unroll),
        grid=(B, hb),
        in_specs=[pl.BlockSpec((1, S, LANES), lambda b, h: (b, 0, h)),
                  pl.BlockSpec((1, S, LANES), lambda b, h: (b, 0, hb + h)),
                  pl.BlockSpec((1, S, LANES), lambda b, h: (b, 0, 2 * hb + h)),
                  pl.BlockSpec((3, 3, 2, tq, win), lambda b, h: (0, 0, h, 0, 0)),
                  pspec, pspec, pspec, pspec],
        out_specs=pl.BlockSpec((1, S, LANES), lambda b, h: (b, 0, h)),
        out_shape=jax.ShapeDtypeStruct((B, S, W_A), BF16),
        scratch_shapes=[pltpu.VMEM((S, LANES), BF16)] * 3 + [pltpu.VMEM((S, LANES), F32)] * 4,
        compiler_params=_params(("parallel", "parallel")),
        name="dilated_attn",
    )(zv, zv, zv, bias, p4, p4.T, p16, p16.T)
    return out.reshape(B * S, W_A)


def _lru_kernel(xr_ref, gr_ref, cw_ref, cb_ref, wa_ref, ba_ref, wx_ref, bx_ref, lc_ref, o_ref, hf_ref, *, S, C, unroll):
    nC = S // C
    row = lax.broadcasted_iota(jnp.int32, (C, LANES), 0)
    E = C + 16

    def conv_chunk(c):
        t0 = pl.multiple_of(c * C, C)
        cur = xr_ref[0, pl.ds(t0, C), :]
        pstart = pl.multiple_of(jnp.maximum(t0 - 8, 0), 8)
        nstart = pl.multiple_of(jnp.minimum(t0 + C, S - 8), 8)
        prev8 = jnp.where(c > 0, xr_ref[0, pl.ds(pstart, 8), :], 0.0)
        next8 = jnp.where(c < nC - 1, xr_ref[0, pl.ds(nstart, 8), :], 0.0)
        ext = jnp.concatenate([prev8, cur, next8], axis=0)
        xc = cb_ref[...] + ext[8:8 + C] * cw_ref[1:2, :]
        xc += pltpu.roll(ext, 1, 0)[8:8 + C] * cw_ref[0:1, :]
        xc += pltpu.roll(ext, E - 1, 0)[8:8 + C] * cw_ref[2:3, :]
        xc += pltpu.roll(ext, E - 2, 0)[8:8 + C] * cw_ref[3:4, :]
        return t0, xc

    def gates(xc, d):
        xb = xc.astype(BF16)
        r = jax.nn.sigmoid(jnp.dot(xb, wa_ref[d], preferred_element_type=F32) + ba_ref[d])
        i = jax.nn.sigmoid(jnp.dot(xb, wx_ref[d], preferred_element_type=F32) + bx_ref[d])
        log_a = lc_ref[d] * r
        a = jnp.exp(log_a)
        u = jnp.sqrt(-jnp.tanh(log_a) * (a * a + 1.0)) * (i * xc)
        return a, u

    def scan(a, u, reverse):
        k = 1
        while k < C:
            shift = C - k if reverse else k
            valid = (row < C - k) if reverse else (row >= k)
            u = u + a * jnp.where(valid, pltpu.roll(u, shift, 0), 0.0)
            a = a * jnp.where(valid, pltpu.roll(a, shift, 0), 1.0)
            k *= 2
        return a, u

    def fwd(c, carry):
        t0, xc = conv_chunk(c)
        a, u = scan(*gates(xc, 0), False)
        h = u + a * carry
        hf_ref[pl.ds(t0, C), :] = h
        return h[C - 1:C, :]

    lax.fori_loop(0, nC, fwd, jnp.zeros((1, LANES), F32), unroll=unroll)

    def bwd(j, carry):
        t0, xc = conv_chunk(nC - 1 - j)
        a, u = scan(*gates(xc, 1), True)
        h = u + a * carry
        g = jax.nn.gelu(gr_ref[0, pl.ds(t0, C), :])
        o_ref[0, pl.ds(t0, C), :] = (g * (hf_ref[pl.ds(t0, C), :] + h)).astype(o_ref.dtype)
        return h[0:1, :]

    lax.fori_loop(0, nC, bwd, jnp.zeros((1, LANES), F32), unroll=unroll)


def rg_lru(xg, conv_w, conv_b, wa, ba, wx, bx, lam, B, S, C=128, unroll=4):
    assert S % C == 0
    xv = xg.reshape(B, S, 2 * D_RNN)
    lc = (-LRU_C * jax.nn.softplus(-lam)).reshape(2, LRU_BLOCKS, 1, LRU_BW)
    vec = lambda t: t.reshape(2, LRU_BLOCKS, 1, LRU_BW)
    wspec = pl.BlockSpec((2, None, LRU_BW, LRU_BW), lambda b, n: (0, n, 0, 0))
    vspec = pl.BlockSpec((2, None, 1, LRU_BW), lambda b, n: (0, n, 0, 0))
    out = pl.pallas_call(
        functools.partial(_lru_kernel, S=S, C=C, unroll=unroll),
        grid=(B, LRU_BLOCKS),
        in_specs=[pl.BlockSpec((1, S, LRU_BW), lambda b, n: (b, 0, n)),
                  pl.BlockSpec((1, S, LRU_BW), lambda b, n: (b, 0, LRU_BLOCKS + n)),
                  pl.BlockSpec((CONV_W, LRU_BW), lambda b, n: (0, n)),
                  pl.BlockSpec((1, LRU_BW), lambda b, n: (0, n)),
                  wspec, vspec, wspec, vspec, vspec],
        out_specs=pl.BlockSpec((1, S, LRU_BW), lambda b, n: (b, 0, n)),
        out_shape=jax.ShapeDtypeStruct((B, S, D_RNN), BF16),
        scratch_shapes=[pltpu.VMEM((S, LRU_BW), F32)],
        compiler_params=_params(("parallel", "parallel")),
        name="rg_lru",
    )(xv, xv, conv_w, conv_b.reshape(1, D_RNN), wa.astype(BF16), vec(ba), wx.astype(BF16), vec(bx), lc)
    return out.reshape(B * S, D_RNN)


BIAS_SATURATION = MAX_DISTANCE // 2 + 1


def _diff_bias(table, tq, tk, e_lo, e_hi):
    rel = (jnp.arange(e_lo, e_hi + 1, dtype=jnp.int32)[:, None] * tk - (tq - 1)
           + jnp.arange(tq + tk, dtype=jnp.int32)[None, :])
    return _toeplitz((table[_t5_bucket(rel)].astype(F32) * LOG2E).transpose(2, 0, 1), tq, tk)


def _lane_tile(x, n):
    return x if n == 1 else jnp.concatenate([x] * n, axis=1)


def _diff_kernel(lam_ref, q_ref, k_ref, v_ref, b_hbm, g_ref, o_ref, m_sc, l_sc, acc_sc, b_buf, b_sem, *, tk, e_lo, e_hi,
                 mult, rs, unroll):
    h, qi = pl.program_id(1), pl.program_id(2)
    t, S = q_ref.shape[1], k_ref.shape[1]
    nk, u = S // tk, t // tk
    tile_of = lambda j: jnp.clip(j - u * qi, e_lo, e_hi) - e_lo
    bias_copy = lambda j, slot: pltpu.make_async_copy(b_hbm.at[h, tile_of(j)], b_buf.at[slot], b_sem.at[slot])

    m_sc[...] = jnp.full(m_sc.shape, -jnp.inf, F32)
    l_sc[...] = jnp.zeros(l_sc.shape, F32)
    acc_sc[...] = jnp.zeros(acc_sc.shape, F32)
    bias_copy(0, 0).start()
    bias_copy(0, 0).wait()

    def key_step(j, slot):
        k0 = pl.multiple_of(j * tk, tk)
        fetch = jnp.logical_and(j + 1 < nk, tile_of(j + 1) != tile_of(j))

        @pl.when(fetch)
        def _():
            bias_copy(j + 1, 1 - slot).start()

        def sub(r, carry):
            rows = pl.ds(pl.multiple_of(r * rs, rs), rs)
            q = q_ref[0, rows, :]
            bias = b_buf[slot, rows, :]
            for mi in range(2):
                sl = slice(mi * DK_C, (mi + 1) * DK_C)
                s = lax.dot_general(q[:, sl], k_ref[0, pl.ds(k0, tk), sl], (((1,), (1,)), ((), ())),
                                    preferred_element_type=F32) + bias
                m_prev = m_sc[mi, rows, :]
                m_new = jnp.maximum(m_prev, jnp.max(s, axis=-1, keepdims=True))
                alpha = jnp.exp2(m_prev - m_new)
                p = jnp.exp2(s - _lane_tile(m_new, tk // LANES))
                l_sc[mi, rows, :] = alpha * l_sc[mi, rows, :] + jnp.sum(p, axis=-1, keepdims=True)
                acc_sc[mi, rows, :] = (_lane_tile(alpha, DV_C // LANES) * acc_sc[mi, rows, :]
                                       + jnp.dot(p.astype(BF16), v_ref[0, pl.ds(k0, tk), :],
                                                 preferred_element_type=F32))
                m_sc[mi, rows, :] = m_new
            return carry

        lax.fori_loop(0, t // rs, sub, 0, unroll=min(unroll, t // rs))

        @pl.when(fetch)
        def _():
            bias_copy(j + 1, 1 - slot).wait()

        return jnp.where(fetch, 1 - slot, slot)

    lax.fori_loop(0, nk, key_step, 0)

    rep = lambda x: _lane_tile(x, DV_C // LANES)
    o = acc_sc[0] / rep(l_sc[0]) - lam_ref[0, 0] * (acc_sc[1] / rep(l_sc[1]))
    o_ref[0] = (_rms(o, g_ref[...]) * mult).astype(o_ref.dtype)


def diff_bias_tiles(table, tq=2048, tk=512):
    e_hi = -(-(BIAS_SATURATION + tq - 1) // tk)
    e_lo = -(-(-(BIAS_SATURATION + tk - 1) // tk))
    return _diff_bias(table, tq, tk, e_lo, e_hi), e_lo, e_hi


def diff_attention(z, bias_tiles, lam, subln, lambda_init, B, S, rs=128, unroll=16):
    bias, e_lo, e_hi = bias_tiles
    tq, tk = bias.shape[2:]
    assert S % tq == 0 and S % tk == 0 and tq % tk == 0 and tq % rs == 0
    zv = z.reshape(B, S, z.shape[1])
    blk = 2 * DK_C
    assert blk == DV_C
    kb, vb = H_C, 2 * H_C
    out = pl.pallas_call(
        functools.partial(_diff_kernel, tk=tk, e_lo=e_lo, e_hi=e_hi, mult=1.0 - lambda_init, rs=rs, unroll=unroll),
        grid=(B, H_C, S // tq),
        in_specs=[pl.BlockSpec(memory_space=pltpu.SMEM),
                  pl.BlockSpec((1, tq, blk), lambda b, h, i: (b, i, h)),
                  pl.BlockSpec((1, S, blk), lambda b, h, i: (b, 0, kb + h)),
                  pl.BlockSpec((1, S, DV_C), lambda b, h, i: (b, 0, vb + h)),
                  pl.BlockSpec(memory_space=pl.ANY),
                  pl.BlockSpec((1, DV_C), lambda b, h, i: (0, 0))],
        out_specs=pl.BlockSpec((1, tq, DV_C), lambda b, h, i: (b, i, h)),
        out_shape=jax.ShapeDtypeStruct((B, S, H_C * DV_C), BF16),
        scratch_shapes=[pltpu.VMEM((2, tq, LANES), F32), pltpu.VMEM((2, tq, LANES), F32),
                        pltpu.VMEM((2, tq, DV_C), F32), pltpu.VMEM((2, tq, tk), F32),
                        pltpu.SemaphoreType.DMA((2,))],
        compiler_params=_params(("parallel", "parallel", "arbitrary")),
        name="diff_attn",
    )(lam.reshape(1, 1), zv, zv, zv, bias, subln.reshape(1, DV_C))
    return out.reshape(B * S, H_C * DV_C)


def _matmul_res_kernel(a_ref, w_ref, x_ref, o_ref):
    o_ref[...] = x_ref[...] + jnp.dot(a_ref[...], w_ref[...], preferred_element_type=F32)


def matmul_residual(a, w, x, tm):
    T, D = x.shape
    K = a.shape[1]
    tm = min(tm, T)
    assert T % tm == 0
    return pl.pallas_call(
        _matmul_res_kernel,
        grid=(T // tm,),
        in_specs=[pl.BlockSpec((tm, K), lambda i: (i, 0)),
                  pl.BlockSpec((K, D), lambda i: (0, 0)),
                  pl.BlockSpec((tm, D), lambda i: (i, 0))],
        out_specs=pl.BlockSpec((tm, D), lambda i: (i, 0)),
        out_shape=jax.ShapeDtypeStruct((T, D), F32),
        compiler_params=_params(("parallel",)),
        name="matmul_residual",
    )(a, w, x)


def _cross_kernel(x_ref, g_ref, wq_ref, kv_ref, wo_ref, o_ref):
    x = x_ref[0]
    h = _rms(x, g_ref[...]).astype(BF16)
    q = jnp.dot(h, wq_ref[...], preferred_element_type=F32).astype(BF16)
    kv = kv_ref[0]
    hw = H_X * DH_X
    outs = []
    for hd in range(H_X):
        sl = slice(hd * DH_X, (hd + 1) * DH_X)
        s = lax.dot_general(q[:, sl], kv[:, sl], (((1,), (1,)), ((), ())), preferred_element_type=F32)
        m = jnp.max(s, axis=-1, keepdims=True)
        p = jnp.exp(s - m)
        l = jnp.sum(p, axis=-1, keepdims=True)
        vh = kv[:, hw + hd * DH_X: hw + (hd + 1) * DH_X]
        outs.append((jnp.dot(p.astype(BF16), vh, preferred_element_type=F32) / l).astype(BF16))
    o = jnp.concatenate(outs, axis=-1)
    o_ref[0] = x + jnp.dot(o, wo_ref[...], preferred_element_type=F32)


def cross_attention(x, g, wq, kvn, wo, B, S, tm):
    D = x.shape[1]
    tm = min(tm, S)
    assert S % tm == 0
    hw = H_X * DH_X
    out = pl.pallas_call(
        _cross_kernel,
        grid=(B, S // tm),
        in_specs=[pl.BlockSpec((1, tm, D), lambda b, i: (b, i, 0)),
                  pl.BlockSpec((1, D), lambda b, i: (0, 0)),
                  pl.BlockSpec((D, hw), lambda b, i: (0, 0)),
                  pl.BlockSpec((1, N_MEM, 2 * hw), lambda b, i: (b, 0, 0)),
                  pl.BlockSpec((hw, D), lambda b, i: (0, 0))],
        out_specs=pl.BlockSpec((1, tm, D), lambda b, i: (b, i, 0)),
        out_shape=jax.ShapeDtypeStruct((B, S, D), F32),
        compiler_params=_params(("parallel", "parallel")),
        name="cross_attn",
    )(x.reshape(B, S, D), g.reshape(1, D), wq, kvn.reshape(B, N_MEM, 2 * hw), wo)
    return out.reshape(B * S, D)


ROUTE_BIG = 1 << 20


def _router_kernel(x_ref, g_ref, w_ref, b_ref, o_ref):
    h = _rms(x_ref[...], g_ref[...])
    h_hi = h.astype(BF16)
    h_lo = (h - h_hi.astype(F32)).astype(BF16)
    a = jnp.dot(h_hi, w_ref[...], preferred_element_type=F32)
    lg = (a[:, :LANES] + a[:, LANES:] + jnp.dot(h_lo, w_ref[:, :LANES], preferred_element_type=F32)) + b_ref[...]
    lane = lax.broadcasted_iota(jnp.int32, lg.shape, 1)
    first = lambda mask: jnp.min(jnp.where(mask, lane, ROUTE_BIG), axis=-1, keepdims=True)
    is_g = lane < N_GROUPS
    gl = jnp.where(is_g, lg, -jnp.inf)
    gmax = jnp.max(gl, axis=-1, keepdims=True)
    gidx = first(gl == gmax)
    pg = 1.0 / jnp.sum(jnp.where(is_g, jnp.exp(lg - gmax), 0.0), axis=-1, keepdims=True)
    e_id = lane - N_GROUPS
    in_grp = (e_id >= 0) & (e_id < N_EXPERTS) & ((e_id // EXPERTS_PER_GROUP) == gidx)
    el = jnp.where(in_grp, lg, -jnp.inf)
    v0 = jnp.max(el, axis=-1, keepdims=True)
    i0 = first(el == v0)
    el2 = jnp.where(lane == i0, -jnp.inf, el)
    v1 = jnp.max(el2, axis=-1, keepdims=True)
    i1 = first(el2 == v1)
    t = jnp.exp(v1 - v0)
    w0 = pg / (1.0 + t)
    w1 = w0 * t
    res = jnp.where(lane == 0, (i0 - N_GROUPS).astype(F32),
                    jnp.where(lane == 1, (i1 - N_GROUPS).astype(F32),
                              jnp.where(lane == 2, w0, jnp.where(lane == 3, w1, 0.0))))
    o_ref[...] = res


def moe_router(x, g, wg, bg, we, be, tm):
    T, D = x.shape
    tm = min(tm, T)
    wr = jnp.zeros((D, LANES), F32).at[:, :N_GROUPS].set(wg).at[:, N_GROUPS:N_GROUPS + N_EXPERTS].set(we)
    br = jnp.zeros((1, LANES), F32).at[0, :N_GROUPS].set(bg).at[0, N_GROUPS:N_GROUPS + N_EXPERTS].set(be)
    w_hi = wr.astype(BF16)
    wr = jnp.concatenate([w_hi, (wr - w_hi.astype(F32)).astype(BF16)], axis=1)
    return pl.pallas_call(
        _router_kernel,
        grid=(T // tm,),
        in_specs=[pl.BlockSpec((tm, D), lambda i: (i, 0)),
                  pl.BlockSpec((1, D), lambda i: (0, 0)),
                  pl.BlockSpec((D, 2 * LANES), lambda i: (0, 0)),
                  pl.BlockSpec((1, LANES), lambda i: (0, 0))],
        out_specs=pl.BlockSpec((tm, LANES), lambda i: (i, 0)),
        out_shape=jax.ShapeDtypeStruct((T, LANES), F32),
        compiler_params=_params(("parallel",)),
        name="moe_router",
    )(x, g.reshape(1, D), wr, br)


GATHER_UNROLL = 8


def _start_row_gather(idx_ref, src_hbm, dst, sem, n):
    def body(r, c):
        pltpu.make_async_copy(src_hbm.at[pl.ds(idx_ref[0, 0, r], 1)], dst.at[pl.ds(r, 1)], sem).start()
        return c

    lax.fori_loop(0, n, body, 0, unroll=GATHER_UNROLL)


def _start_row_gather_unrolled(idx_ref, src_hbm, dst, sem, n):
    for r in range(n):
        pltpu.make_async_copy(src_hbm.at[pl.ds(idx_ref[0, 0, r], 1)], dst.at[pl.ds(r, 1)], sem).start()


def _wait_row_gather(src_hbm, dst, sem, n):
    def body(r, c):
        pltpu.make_async_copy(src_hbm.at[pl.ds(0, 1)], dst.at[pl.ds(0, 1)], sem).wait()
        return c

    lax.fori_loop(0, n, body, 0, unroll=GATHER_UNROLL)


def _expert_kernel(be_ref, nu_ref, tok_ref, tok_next_ref, x_hbm, g_ref, wg_ref, wu_ref, wd_ref, o_ref, xbuf, sem, *, blk):
    i = pl.program_id(0)
    nu = nu_ref[0]
    slot = i % 2

    @pl.when(jnp.logical_and(i == 0, nu > 0))
    def _():
        _start_row_gather(tok_ref, x_hbm, xbuf.at[0], sem.at[0], blk)

    @pl.when(i < nu)
    def _():
        _wait_row_gather(x_hbm, xbuf.at[slot], sem.at[slot], blk)
        _start_row_gather_unrolled(tok_next_ref, x_hbm, xbuf.at[1 - slot], sem.at[1 - slot], blk)
        h = _rms(xbuf[slot], g_ref[...]).astype(BF16)
        a = jnp.dot(h, wg_ref[0], preferred_element_type=F32)
        b = jnp.dot(h, wu_ref[0], preferred_element_type=F32)
        hid = (jax.nn.silu(a) * b).astype(BF16)
        o_ref[...] = jnp.dot(hid, wd_ref[0], preferred_element_type=F32)

    @pl.when(i == nu - 1)
    def _():
        _wait_row_gather(x_hbm, xbuf.at[1 - slot], sem.at[1 - slot], blk)

    @pl.when(i >= nu)
    def _():
        o_ref[...] = jnp.zeros(o_ref.shape, o_ref.dtype)


def moe_experts(x, g, buf_tok, blk_e, n_used, w_gate, w_up, w_down, blk):
    T, D = x.shape
    cap = buf_tok.shape[0]
    n_blk = cap // blk
    tok = buf_tok.reshape(n_blk, 1, blk)
    grid_spec = pltpu.PrefetchScalarGridSpec(
        num_scalar_prefetch=2,
        grid=(n_blk,),
        in_specs=[pl.BlockSpec((1, 1, blk), lambda i, be, nu: (i, 0, 0), memory_space=pltpu.SMEM),
                  pl.BlockSpec((1, 1, blk), lambda i, be, nu: (jnp.minimum(i + 1, jnp.maximum(nu[0] - 1, 0)), 0, 0),
                               memory_space=pltpu.SMEM),
                  pl.BlockSpec(memory_space=pl.ANY),
                  pl.BlockSpec((1, D), lambda i, be, nu: (0, 0)),
                  pl.BlockSpec((1, D, D_FF), lambda i, be, nu: (be[i], 0, 0)),
                  pl.BlockSpec((1, D, D_FF), lambda i, be, nu: (be[i], 0, 0)),
                  pl.BlockSpec((1, D_FF, D), lambda i, be, nu: (be[i], 0, 0))],
        out_specs=pl.BlockSpec((blk, D), lambda i, be, nu: (i, 0)),
        scratch_shapes=[pltpu.VMEM((2, blk, D), F32), pltpu.SemaphoreType.DMA((2,))],
    )
    return pl.pallas_call(
        functools.partial(_expert_kernel, blk=blk),
        grid_spec=grid_spec,
        out_shape=jax.ShapeDtypeStruct((cap, D), F32),
        compiler_params=_params(("arbitrary",)),
        name="moe_experts",
    )(blk_e, n_used, tok, tok, x, g.reshape(1, D), w_gate, w_up, w_down)


def _combine_kernel(d_ref, d_next_ref, x_ref, r_ref, y_hbm, g_ref, o_ref, buf, sem, *, tt, final):
    i = pl.program_id(0)
    slot = i % 2

    @pl.when(i == 0)
    def _():
        _start_row_gather(d_ref, y_hbm, buf.at[0], sem.at[0], 2 * tt)

    _wait_row_gather(y_hbm, buf.at[slot], sem.at[slot], 2 * tt)
    _start_row_gather_unrolled(d_next_ref, y_hbm, buf.at[1 - slot], sem.at[1 - slot], 2 * tt)
    r = r_ref[...]
    y = x_ref[...] + r[:, 2:3] * buf[slot, 0:tt, :] + r[:, 3:4] * buf[slot, tt:2 * tt, :]
    o_ref[...] = _rms(y, g_ref[...]) if final else y

    @pl.when(i == pl.num_programs(0) - 1)
    def _():
        _wait_row_gather(y_hbm, buf.at[1 - slot], sem.at[1 - slot], 2 * tt)


def moe_combine(x, routed, dest, y, g_final, tt):
    T, D = x.shape
    n = T // tt
    d = dest.reshape(n, tt, 2).transpose(0, 2, 1).reshape(n, 1, 2 * tt)
    final = g_final is not None
    g = (g_final if final else jnp.ones((D,), F32)).reshape(1, D)
    return pl.pallas_call(
        functools.partial(_combine_kernel, tt=tt, final=final),
        grid=(n,),
        in_specs=[pl.BlockSpec((1, 1, 2 * tt), lambda i: (i, 0, 0), memory_space=pltpu.SMEM),
                  pl.BlockSpec((1, 1, 2 * tt), lambda i: (jnp.minimum(i + 1, n - 1), 0, 0), memory_space=pltpu.SMEM),
                  pl.BlockSpec((tt, D), lambda i: (i, 0)),
                  pl.BlockSpec((tt, LANES), lambda i: (i, 0)),
                  pl.BlockSpec(memory_space=pl.ANY),
                  pl.BlockSpec((1, D), lambda i: (0, 0))],
        out_specs=pl.BlockSpec((tt, D), lambda i: (i, 0)),
        out_shape=jax.ShapeDtypeStruct((T, D), F32),
        scratch_shapes=[pltpu.VMEM((2, 2 * tt, D), F32), pltpu.SemaphoreType.DMA((2,))],
        compiler_params=_params(("arbitrary",)),
        name="moe_combine",
    )(d, d, x, routed, y, g)


def _route_plan(routed, blk):
    T = routed.shape[0]
    e = routed[:, :2].astype(jnp.int32)
    flat_e = e.reshape(-1)
    n_rows = 2 * T
    onehot = (flat_e[:, None] == jnp.arange(N_EXPERTS, dtype=jnp.int32)[None, :]).astype(jnp.int32)
    cum = jnp.cumsum(onehot, axis=0)
    rank = jnp.sum(cum * onehot, axis=1) - 1
    counts = cum[-1]
    padded = (counts + blk - 1) // blk * blk
    pad_end = jnp.cumsum(padded)
    pad_start = pad_end - padded
    dest = pad_start[flat_e] + rank
    n_blk = -(-n_rows // blk) + N_EXPERTS
    cap = n_blk * blk
    flat_t = jnp.repeat(jnp.arange(T, dtype=jnp.int32), 2)
    buf_tok = jnp.zeros((cap,), jnp.int32).at[dest].set(flat_t)
    blk_e = jnp.minimum(jnp.searchsorted(pad_end, jnp.arange(n_blk, dtype=jnp.int32) * blk, side='right'),
                        N_EXPERTS - 1).astype(jnp.int32)
    n_used = (pad_end[-1:] // blk).astype(jnp.int32)
    return buf_tok, dest.reshape(T, 2), blk_e, n_used


def hier_moe(x, g, wg, bg, we, be, w_gate, w_up, w_down, g_final=None, blk=256, tm=512, tt=256):
    routed = moe_router(x, g, wg, bg, we, be, tm)
    buf_tok, dest, blk_e, n_used = _route_plan(routed, blk)
    y = moe_experts(x, g, buf_tok, blk_e, n_used, w_gate, w_up, w_down, blk)
    return moe_combine(x, routed, dest, y, g_final, min(tt, x.shape[0]))


def _lambda_init(layer):
    return 0.8 - 0.6 * math.exp(-0.3 * layer)


def _prep_weights(p):
    w = dict(p)
    col = lambda n: jnp.arange(n, dtype=jnp.int32)
    ab = p["ab_w_in"]
    w["ab_w_in"] = (ab * jnp.where(col(ab.shape[-1]) < W_A, DH_A ** -0.5, 1.0)).astype(BF16)
    w["ab_w_out"] = p["ab_w_out"].astype(BF16)
    cw = p["c_w_in"]
    w["c_w_in"] = (cw * jnp.where(col(cw.shape[-1]) < H_C * 2 * DK_C, DK_C ** -0.5 * LOG2E, 1.0)).astype(BF16)
    w["c_w_out"] = p["c_w_out"].astype(BF16)
    w["x_wq"] = (p["x_wq"] * DH_X ** -0.5).astype(BF16)
    w["x_wkv"] = p["x_wkv"].astype(BF16)
    w["x_wo"] = p["x_wo"].astype(BF16)
    for n in ("moe_w_gate", "moe_w_up", "moe_w_down"):
        w[n] = p[n].astype(BF16)
    lv = p["c_lam"].astype(F32)
    w["c_lam_scalar"] = [jnp.exp(jnp.sum(lv[i, 0] * lv[i, 1])) - jnp.exp(jnp.sum(lv[i, 2] * lv[i, 3]))
                         + _lambda_init(2 * i + 1) for i in range(lv.shape[0])]
    w["dil_bias"] = dilated_bias_tiles(p["rel_bias"][:, :H_A])
    w["diff_bias"] = diff_bias_tiles(p["rel_bias"][:, H_A:])
    return w


def _trunk(x3, mem3, w):
    B, S, D = x3.shape
    x = x3.reshape(B * S, D)
    mem = mem3.reshape(B * N_MEM, D)
    for l in range(DEPTH):
        i = l // 2
        if l % 2 == 0:
            zqkv = norm_matmul(x, w["norm_mix"][l], w["ab_w_in"][i], BF16, 1024, 1024, 0, 3 * W_A)
            xg = norm_matmul(x, w["norm_mix"][l], w["ab_w_in"][i], F32, 1024, 1024, 3 * W_A, 2 * D_RNN)
            attn = dilated_attention_fused(zqkv, w["dil_bias"], B, S)
            rec = rg_lru(xg, w["ab_conv_w"][i], w["ab_conv_b"][i], w["lru_wa"][i], w["lru_ba"][i],
                         w["lru_wx"][i], w["lru_bx"][i], w["lru_lam"][i], B, S)
            x = mix_out(attn, rec, w["ab_w_out"][i], x, 512)
        else:
            z = norm_matmul(x, w["norm_mix"][l], w["c_w_in"][i], BF16, 1024, 1024)
            o = diff_attention(z, w["diff_bias"], w["c_lam_scalar"][i], w["c_subln"][i], _lambda_init(l), B, S)
            x = matmul_residual(o, w["c_w_out"][i], x, 512)
        kvn = norm_matmul(mem, w["norm_mem"][l], w["x_wkv"][l], BF16, 1024, 1024)
        x = cross_attention(x, w["norm_cross"][l], w["x_wq"][l], kvn, w["x_wo"][l], B, S, 512)
        x = hier_moe(x, w["norm_ffn"][l], w["moe_wg"][l], w["moe_bg"][l], w["moe_we"][l], w["moe_be"][l],
                     w["moe_w_gate"][l], w["moe_w_up"][l], w["moe_w_down"][l],
                     g_final=w["norm_final"] if l == DEPTH - 1 else None)
    return x.reshape(B, S, D)


def kernel(x_prompt, x_sample, mem_prompt, mem_sample, rel_bias, ab_w_in, ab_conv_w, ab_conv_b, lru_wa, lru_ba, lru_wx, lru_bx, lru_lam, ab_w_out, c_w_in, c_lam, c_subln, c_w_out, norm_mix, norm_cross, norm_mem, x_wq, x_wkv, x_wo, norm_ffn, moe_wg, moe_bg, moe_we, moe_be, moe_w_gate, moe_w_up, moe_w_down, norm_final):
    w = _prep_weights(dict(
        rel_bias=rel_bias, ab_w_in=ab_w_in, ab_conv_w=ab_conv_w, ab_conv_b=ab_conv_b, lru_wa=lru_wa, lru_ba=lru_ba,
        lru_wx=lru_wx, lru_bx=lru_bx, lru_lam=lru_lam, ab_w_out=ab_w_out, c_w_in=c_w_in, c_lam=c_lam,
        c_subln=c_subln, c_w_out=c_w_out, norm_mix=norm_mix, norm_cross=norm_cross, norm_mem=norm_mem, x_wq=x_wq,
        x_wkv=x_wkv, x_wo=x_wo, norm_ffn=norm_ffn, moe_wg=moe_wg, moe_bg=moe_bg, moe_we=moe_we, moe_be=moe_be,
        moe_w_gate=moe_w_gate, moe_w_up=moe_w_up, moe_w_down=moe_w_down, norm_final=norm_final))
    return _trunk(x_prompt, mem_prompt, w), _trunk(x_sample, mem_sample, w)
```

```python
import functools
import math

import jax
import jax.numpy as jnp
from jax import lax
from jax.experimental import pallas as pl
from jax.experimental.pallas import tpu as pltpu

F32 = jnp.float32
BF16 = jnp.bfloat16

D_MODEL = 2048
DEPTH = 4
H_A, DH_A = 16, 64
W_A = H_A * DH_A
DILATED_PATTERNS = ((128, 1), (512, 4), (2048, 16))
D_RNN = 1024
LRU_BLOCKS, LRU_BW = 8, 128
CONV_W = 4
LRU_C = 8.0
H_C, DK_C, DV_C = 8, 128, 256
NUM_BUCKETS, MAX_DISTANCE = 32, 2048
H_X, DH_X, N_MEM = 4, 128, 256
N_GROUPS, EXPERTS_PER_GROUP = 4, 4
N_EXPERTS = N_GROUPS * EXPERTS_PER_GROUP
D_FF = 1024
EPS = 1e-6
NEG = -1e30
LOG2E = math.log2(math.e)

LANES = 128
VMEM_LIMIT = 56 * 1024 * 1024


def _params(sem):
    return pltpu.CompilerParams(dimension_semantics=sem, vmem_limit_bytes=VMEM_LIMIT)


def _rms(x, g):
    return x * lax.rsqrt(jnp.mean(x * x, axis=-1, keepdims=True) + EPS) * g


def _norm_matmul_kernel(x_ref, g_ref, w_ref, o_ref, h_ref):
    @pl.when(pl.program_id(1) == 0)
    def _():
        h_ref[...] = _rms(x_ref[...], g_ref[...]).astype(BF16)

    o_ref[...] = jnp.dot(h_ref[...], w_ref[...], preferred_element_type=F32).astype(o_ref.dtype)


def norm_matmul(x, g, w, out_dtype, tm, tn, col0=0, n_cols=None):
    T, D = x.shape
    N = w.shape[1] - col0 if n_cols is None else n_cols
    tm, tn = min(tm, T), min(tn, N)
    assert T % tm == 0 and N % tn == 0 and col0 % tn == 0
    j0 = col0 // tn
    return pl.pallas_call(
        _norm_matmul_kernel,
        grid=(T // tm, N // tn),
        in_specs=[pl.BlockSpec((tm, D), lambda i, j: (i, 0)),
                  pl.BlockSpec((1, D), lambda i, j: (0, 0)),
                  pl.BlockSpec((D, tn), lambda i, j: (0, j0 + j))],
        out_specs=pl.BlockSpec((tm, tn), lambda i, j: (i, j)),
        out_shape=jax.ShapeDtypeStruct((T, N), out_dtype),
        scratch_shapes=[pltpu.VMEM((tm, D), BF16)],
        compiler_params=_params(("parallel", "arbitrary")),
        name="norm_matmul",
    )(x, g.reshape(1, D), w)


def _mix_out_kernel(a1_ref, a2_ref, w_ref, x_ref, o_ref):
    k1 = a1_ref.shape[1]
    acc = jnp.dot(a1_ref[...], w_ref[0:k1, :], preferred_element_type=F32)
    acc += jnp.dot(a2_ref[...], w_ref[k1:, :], preferred_element_type=F32)
    o_ref[...] = x_ref[...] + acc


def mix_out(a1, a2, w, x, tm):
    T, D = x.shape
    K1, K2 = a1.shape[1], a2.shape[1]
    tm = min(tm, T)
    assert T % tm == 0
    return pl.pallas_call(
        _mix_out_kernel,
        grid=(T // tm,),
        in_specs=[pl.BlockSpec((tm, K1), lambda i: (i, 0)),
                  pl.BlockSpec((tm, K2), lambda i: (i, 0)),
                  pl.BlockSpec((K1 + K2, D), lambda i: (0, 0)),
                  pl.BlockSpec((tm, D), lambda i: (i, 0))],
        out_specs=pl.BlockSpec((tm, D), lambda i: (i, 0)),
        out_shape=jax.ShapeDtypeStruct((T, D), F32),
        compiler_params=_params(("parallel",)),
        name="mix_out",
    )(a1, a2, w, x)


def _t5_bucket(rel):
    half = NUM_BUCKETS // 2
    exact = half // 2
    n = jnp.abs(rel)
    large = exact + (jnp.log(jnp.maximum(n, 1).astype(F32) / exact)
                     / math.log(MAX_DISTANCE / exact) * (half - exact)).astype(jnp.int32)
    large = jnp.minimum(large, half - 1)
    return (rel > 0).astype(jnp.int32) * half + jnp.where(n < exact, n, large)


DIL_RADIUS = 64


def _dilated_bias(table, dil, tq):
    win = tq + 2 * DIL_RADIUS
    offs = jnp.array((0, -DIL_RADIUS, -2 * DIL_RADIUS), jnp.int32)
    rel = offs[:, None] - (tq - 1) + jnp.arange(tq + win, dtype=jnp.int32)[None, :]
    w = jnp.where((jnp.abs(rel) <= DIL_RADIUS)[..., None], table[_t5_bucket(rel * dil)].astype(F32), NEG)
    return _toeplitz(w.transpose(0, 2, 1), tq, win)


def _toeplitz_kernel(w_ref, o_ref):
    tq, tk = o_ref.shape[2], o_ref.shape[3]
    x = jnp.broadcast_to(w_ref[0, 0], (tq, w_ref.shape[3]))
    o_ref[0, 0] = pltpu.roll(x, 0, 1, stride=1, stride_axis=0)[:, :tk]


def _toeplitz(w, tq, tk):
    a, b, wd = w.shape
    assert wd == tq + tk and wd % LANES == 0
    w = jnp.roll(w, -(tq - 1), axis=-1).reshape(a, b, 1, wd)
    return pl.pallas_call(
        _toeplitz_kernel,
        grid=(a, b),
        in_specs=[pl.BlockSpec((1, 1, 1, wd), lambda i, j: (i, j, 0, 0))],
        out_specs=pl.BlockSpec((1, 1, tq, tk), lambda i, j: (i, j, 0, 0)),
        out_shape=jax.ShapeDtypeStruct((a, b, tq, tk), F32),
        compiler_params=_params(("parallel", "parallel")),
        name="toeplitz_bias",
    )(w)


PERM_BLOCK = 256


def _residue_perm(dil):
    n_l = PERM_BLOCK // dil
    i = jnp.arange(PERM_BLOCK, dtype=jnp.int32)
    src = (i % n_l) * dil + i // n_l
    return (src[:, None] == i[None, :]).astype(BF16)


def _dilated_fused_kernel(q_ref, k_ref, v_ref, b_ref, p4_ref, p4t_ref, p16_ref, p16t_ref, o_ref,
                          qp, kp, vp, op, lp, osc, lsc, *, S, tq, unroll):
    win = tq + 2 * DIL_RADIUS
    head0 = lax.broadcasted_iota(jnp.int32, (1, LANES), 1) < DH_A

    def attend(qr, kr, vr, g, L, n_sub, out_o, out_l):
        nq = L // tq

        def body(n, carry):
            sub = n // nq
            i = n - sub * nq
            base = sub * L
            i0 = i * tq
            start = jnp.clip(i0 - DIL_RADIUS, 0, L - win)
            var = jnp.where(i == 0, 0, jnp.where(i == nq - 1, 2, 1))
            rows = pl.ds(pl.multiple_of(base + i0, tq), tq)
            krows = pl.ds(pl.multiple_of(base + start, DIL_RADIUS), win)
            q, k, v = qr[rows, :], kr[krows, :], vr[krows, :]
            outs, lses = [], []
            for h in range(2):
                qh = jnp.where(head0 if h == 0 else jnp.logical_not(head0), q, jnp.zeros_like(q))
                s = lax.dot_general(qh, k, (((1,), (1,)), ((), ())), preferred_element_type=F32)
                s = s + b_ref[g, var, h]
                m = jnp.max(s, axis=-1, keepdims=True)
                p = jnp.exp(s - m)
                l = jnp.sum(p, axis=-1, keepdims=True)
                outs.append(jnp.dot(p.astype(BF16), v, preferred_element_type=F32) / l)
                lses.append(m + jnp.log(l))
            out_o[rows, :] = jnp.where(head0, outs[0], outs[1])
            out_l[rows, :] = jnp.where(head0, lses[0], lses[1])
            return carry

        lax.fori_loop(0, n_sub * nq, body, 0, unroll=min(unroll, n_sub * nq))

    def regroup(p_ref, dil):
        n_l, L = PERM_BLOCK // dil, S // dil

        def body(c, carry):
            r0 = pl.multiple_of(c * PERM_BLOCK, PERM_BLOCK)
            x = jnp.concatenate([src[0, pl.ds(r0, PERM_BLOCK), :] for src in (q_ref, k_ref, v_ref)], axis=1)
            y = jnp.dot(p_ref[...], x, preferred_element_type=F32).astype(BF16)
            for j, dst in enumerate((qp, kp, vp)):
                for r in range(dil):
                    dst[pl.ds(pl.multiple_of(r * L + c * n_l, n_l), n_l), :] = (
                        y[r * n_l:(r + 1) * n_l, j * LANES:(j + 1) * LANES])
            return carry

        lax.fori_loop(0, S // PERM_BLOCK, body, 0, unroll=4)

    def merge_back(pt_ref, dil, last):
        n_l, L = PERM_BLOCK // dil, S // dil

        def body(c, carry):
            piece = lambda ref: jnp.concatenate(
                [ref[pl.ds(pl.multiple_of(r * L + c * n_l, n_l), n_l), :] for r in range(dil)], axis=0)
            x = piece(lp)
            x1 = x.astype(BF16)
            x2 = (x - x1.astype(F32)).astype(BF16)
            x3 = (x - x1.astype(F32) - x2.astype(F32)).astype(BF16)
            back = jnp.dot(pt_ref[...], jnp.concatenate([piece(op).astype(BF16), x1, x2, x3], axis=1),
                           preferred_element_type=F32)
            o_g = back[:, :LANES]
            lse_g = back[:, LANES:2 * LANES] + back[:, 2 * LANES:3 * LANES] + back[:, 3 * LANES:]
            rows = pl.ds(pl.multiple_of(c * PERM_BLOCK, PERM_BLOCK), PERM_BLOCK)
            lse0 = lsc[rows, :]
            m = jnp.maximum(lse0, lse_g)
            w0, w1 = jnp.exp(lse0 - m), jnp.exp(lse_g - m)
            den = w0 + w1
            o_new = (w0 * osc[rows, :] + w1 * o_g) / den
            if last:
                o_ref[0, rows, :] = o_new.astype(o_ref.dtype)
            else:
                osc[rows, :] = o_new
                lsc[rows, :] = m + jnp.log(den)
            return carry

        lax.fori_loop(0, S // PERM_BLOCK, body, 0, unroll=4)

    attend(q_ref.at[0], k_ref.at[0], v_ref.at[0], 0, S, 1, osc, lsc)
    regroup(p4_ref, 4)
    attend(qp, kp, vp, 1, S // 4, 4, op, lp)
    merge_back(p4t_ref, 4, False)
    regroup(p16_ref, 16)
    attend(qp, kp, vp, 2, S // 16, 16, op, lp)
    merge_back(p16t_ref, 16, True)


DIL_TQ = 128


def dilated_bias_tiles(table):
    return jnp.stack([_dilated_bias(table, d, DIL_TQ) for _, d in DILATED_PATTERNS])


def dilated_attention_fused(zqkv, bias, B, S, unroll=8):
    tq = DIL_TQ
    dils = tuple(d for _, d in DILATED_PATTERNS)
    assert dils == (1, 4, 16) and S % PERM_BLOCK == 0 and (S // 16) % tq == 0 and S // 16 >= tq + 2 * DIL_RADIUS
    zv = zqkv.reshape(B, S, 3 * W_A)
    hb = W_A // LANES
    win = tq + 2 * DIL_RADIUS
    p4, p16 = _residue_perm(4), _residue_perm(16)
    pspec = pl.BlockSpec((PERM_BLOCK, PERM_BLOCK), lambda b, h: (0, 0))
    out = pl.pallas_call(
        functools.partial(_dilated_fused_kernel, S=S, tq=tq, unroll=unroll),
        grid=(B, hb),
        in_specs=[pl.BlockSpec((1, S, LANES), lambda b, h: (b, 0, h)),
                  pl.BlockSpec((1, S, LANES), lambda b, h: (b, 0, hb + h)),
                  pl.BlockSpec((1, S, LANES), lambda b, h: (b, 0, 2 * hb + h)),
                  pl.BlockSpec((3, 3, 2, tq, win), lambda b, h: (0, 0, h, 0, 0)),
                  pspec, pspec, pspec, pspec],
        out_specs=pl.BlockSpec((1, S, LANES), lambda b, h: (b, 0, h)),
        out_shape=jax.ShapeDtypeStruct((B, S, W_A), BF16),
        scratch_shapes=[pltpu.VMEM((S, LANES), BF16)] * 3 + [pltpu.VMEM((S, LANES), F32)] * 4,
        compiler_params=_params(("parallel", "parallel")),
        name="dilated_attn",
    )(zv, zv, zv, bias, p4, p4.T, p16, p16.T)
    return out.reshape(B * S, W_A)


def _lru_kernel(xr_ref, gr_ref, cw_ref, cb_ref, wa_ref, ba_ref, wx_ref, bx_ref, lc_ref, o_ref, hf_ref, *, S, C, unroll):
    nC = S // C
    row = lax.broadcasted_iota(jnp.int32, (C, LANES), 0)
    E = C + 16

    def conv_chunk(c):
        t0 = pl.multiple_of(c * C, C)
        cur = xr_ref[0, pl.ds(t0, C), :]
        pstart = pl.multiple_of(jnp.maximum(t0 - 8, 0), 8)
        nstart = pl.multiple_of(jnp.minimum(t0 + C, S - 8), 8)
        prev8 = jnp.where(c > 0, xr_ref[0, pl.ds(pstart, 8), :], 0.0)
        next8 = jnp.where(c < nC - 1, xr_ref[0, pl.ds(nstart, 8), :], 0.0)
        ext = jnp.concatenate([prev8, cur, next8], axis=0)
        xc = cb_ref[...] + ext[8:8 + C] * cw_ref[1:2, :]
        xc += pltpu.roll(ext, 1, 0)[8:8 + C] * cw_ref[0:1, :]
        xc += pltpu.roll(ext, E - 1, 0)[8:8 + C] * cw_ref[2:3, :]
        xc += pltpu.roll(ext, E - 2, 0)[8:8 + C] * cw_ref[3:4, :]
        return t0, xc

    def gates(xc, d):
        xb = xc.astype(BF16)
        r = jax.nn.sigmoid(jnp.dot(xb, wa_ref[d], preferred_element_type=F32) + ba_ref[d])
        i = jax.nn.sigmoid(jnp.dot(xb, wx_ref[d], preferred_element_type=F32) + bx_ref[d])
        log_a = lc_ref[d] * r
        a = jnp.exp(log_a)
        u = jnp.sqrt(-jnp.tanh(log_a) * (a * a + 1.0)) * (i * xc)
        return a, u

    def scan(a, u, reverse):
        k = 1
        while k < C:
            shift = C - k if reverse else k
            valid = (row < C - k) if reverse else (row >= k)
            u = u + a * jnp.where(valid, pltpu.roll(u, shift, 0), 0.0)
            a = a * jnp.where(valid, pltpu.roll(a, shift, 0), 1.0)
            k *= 2
        return a, u

    def fwd(c, carry):
        t0, xc = conv_chunk(c)
        a, u = scan(*gates(xc, 0), False)
        h = u + a * carry
        hf_ref[pl.ds(t0, C), :] = h
        return h[C - 1:C, :]

    lax.fori_loop(0, nC, fwd, jnp.zeros((1, LANES), F32), unroll=unroll)

    def bwd(j, carry):
        t0, xc = conv_chunk(nC - 1 - j)
        a, u = scan(*gates(xc, 1), True)
        h = u + a * carry
        g = jax.nn.gelu(gr_ref[0, pl.ds(t0, C), :])
        o_ref[0, pl.ds(t0, C), :] = (g * (hf_ref[pl.ds(t0, C), :] + h)).astype(o_ref.dtype)
        return h[0:1, :]

    lax.fori_loop(0, nC, bwd, jnp.zeros((1, LANES), F32), unroll=unroll)


def rg_lru(xg, conv_w, conv_b, wa, ba, wx, bx, lam, B, S, C=128, unroll=4):
    assert S % C == 0
    xv = xg.reshape(B, S, 2 * D_RNN)
    lc = (-LRU_C * jax.nn.softplus(-lam)).reshape(2, LRU_BLOCKS, 1, LRU_BW)
    vec = lambda t: t.reshape(2, LRU_BLOCKS, 1, LRU_BW)
    wspec = pl.BlockSpec((2, None, LRU_BW, LRU_BW), lambda b, n: (0, n, 0, 0))
    vspec = pl.BlockSpec((2, None, 1, LRU_BW), lambda b, n: (0, n, 0, 0))
    out = pl.pallas_call(
        functools.partial(_lru_kernel, S=S, C=C, unroll=unroll),
        grid=(B, LRU_BLOCKS),
        in_specs=[pl.BlockSpec((1, S, LRU_BW), lambda b, n: (b, 0, n)),
                  pl.BlockSpec((1, S, LRU_BW), lambda b, n: (b, 0, LRU_BLOCKS + n)),
                  pl.BlockSpec((CONV_W, LRU_BW), lambda b, n: (0, n)),
                  pl.BlockSpec((1, LRU_BW), lambda b, n: (0, n)),
                  wspec, vspec, wspec, vspec, vspec],
        out_specs=pl.BlockSpec((1, S, LRU_BW), lambda b, n: (b, 0, n)),
        out_shape=jax.ShapeDtypeStruct((B, S, D_RNN), BF16),
        scratch_shapes=[pltpu.VMEM((S, LRU_BW), F32)],
        compiler_params=_params(("parallel", "parallel")),
        name="rg_lru",
    )(xv, xv, conv_w, conv_b.reshape(1, D_RNN), wa.astype(BF16), vec(ba), wx.astype(BF16), vec(bx), lc)
    return out.reshape(B * S, D_RNN)


BIAS_SATURATION = MAX_DISTANCE // 2 + 1


def _diff_bias(table, tq, tk, e_lo, e_hi):
    rel = (jnp.arange(e_lo, e_hi + 1, dtype=jnp.int32)[:, None] * tk - (tq - 1)
           + jnp.arange(tq + tk, dtype=jnp.int32)[None, :])
    return _toeplitz((table[_t5_bucket(rel)].astype(F32) * LOG2E).transpose(2, 0, 1), tq, tk)


def _lane_tile(x, n):
    return x if n == 1 else jnp.concatenate([x] * n, axis=1)


def _diff_kernel(lam_ref, cb_ref, q_ref, k_ref, v_ref, b_ref, g_ref, o_ref, m_sc, l_sc, acc_sc, *, nk, e_lo, e_hi, mult, rs,
                 unroll):
    kv = pl.program_id(3)
    t = q_ref.shape[1]
    tk = k_ref.shape[1]
    e = kv - (t // tk) * pl.program_id(2)

    @pl.when(kv == 0)
    def _():
        m_sc[...] = jnp.full(m_sc.shape, -jnp.inf, F32)
        l_sc[...] = jnp.zeros(l_sc.shape, F32)
        acc_sc[...] = jnp.zeros(acc_sc.shape, F32)

    def sub(r, carry, *, const_bias):
        r0 = pl.multiple_of(r * rs, rs)
        rows = pl.ds(r0, rs)
        q = q_ref[0, rows, :]
        bias = cb_ref[pl.program_id(1), (e > 0).astype(jnp.int32)] if const_bias else b_ref[0, 0, rows, :]
        for mi in range(2):
            sl = slice(mi * DK_C, (mi + 1) * DK_C)
            s = lax.dot_general(q[:, sl], k_ref[0, :, sl], (((1,), (1,)), ((), ())), preferred_element_type=F32)
            if not const_bias:
                s = s + bias
            m_cur = jnp.max(s, axis=-1, keepdims=True)
            m_prev = m_sc[mi, rows, :]
            m_new = jnp.maximum(m_prev, m_cur + bias if const_bias else m_cur)
            alpha = jnp.exp2(m_prev - m_new)
            p = jnp.exp2(s - _lane_tile(m_new - bias if const_bias else m_new, tk // LANES))
            l_sc[mi, rows, :] = alpha * l_sc[mi, rows, :] + jnp.sum(p, axis=-1, keepdims=True)
            acc_sc[mi, rows, :] = (_lane_tile(alpha, DV_C // LANES) * acc_sc[mi, rows, :]
                                   + jnp.dot(p.astype(BF16), v_ref[0], preferred_element_type=F32))
            m_sc[mi, rows, :] = m_new
        return carry

    far = jnp.logical_or(e >= e_hi, e <= e_lo)
    for const_bias in (False, True):
        @pl.when(far == const_bias)
        def _():
            lax.fori_loop(0, t // rs, functools.partial(sub, const_bias=const_bias), 0,
                          unroll=min(unroll, t // rs))

    @pl.when(kv == nk - 1)
    def _():
        rep = lambda x: _lane_tile(x, DV_C // LANES)
        o = acc_sc[0] / rep(l_sc[0]) - lam_ref[0, 0] * (acc_sc[1] / rep(l_sc[1]))
        o_ref[0] = (_rms(o, g_ref[...]) * mult).astype(o_ref.dtype)


def diff_bias_tiles(table, tq=2048, tk=512):
    e_hi = -(-(BIAS_SATURATION + tq - 1) // tk)
    e_lo = -(-(-(BIAS_SATURATION + tk - 1) // tk))
    return _diff_bias(table, tq, tk, e_lo, e_hi), e_lo, e_hi


def diff_attention(z, bias_tiles, lam, subln, lambda_init, B, S, rs=128, unroll=16):
    bias, e_lo, e_hi = bias_tiles
    tq, tk = bias.shape[2:]
    assert S % tq == 0 and S % tk == 0 and tq % tk == 0 and tq % rs == 0
    u = tq // tk
    zv = z.reshape(B, S, z.shape[1])
    blk = 2 * DK_C
    assert blk == DV_C
    kb, vb = H_C, 2 * H_C
    out = pl.pallas_call(
        functools.partial(_diff_kernel, nk=S // tk, e_lo=e_lo, e_hi=e_hi, mult=1.0 - lambda_init, rs=rs, unroll=unroll),
        grid=(B, H_C, S // tq, S // tk),
        in_specs=[pl.BlockSpec(memory_space=pltpu.SMEM),
                  pl.BlockSpec(memory_space=pltpu.SMEM),
                  pl.BlockSpec((1, tq, blk), lambda b, h, i, j: (b, i, h)),
                  pl.BlockSpec((1, tk, blk), lambda b, h, i, j: (b, j, kb + h)),
                  pl.BlockSpec((1, tk, DV_C), lambda b, h, i, j: (b, j, vb + h)),
                  pl.BlockSpec((1, 1, tq, tk), lambda b, h, i, j: (h, jnp.clip(j - u * i, e_lo, e_hi) - e_lo, 0, 0)),
                  pl.BlockSpec((1, DV_C), lambda b, h, i, j: (0, 0))],
        out_specs=pl.BlockSpec((1, tq, DV_C), lambda b, h, i, j: (b, i, h)),
        out_shape=jax.ShapeDtypeStruct((B, S, H_C * DV_C), BF16),
        scratch_shapes=[pltpu.VMEM((2, tq, LANES), F32), pltpu.VMEM((2, tq, LANES), F32),
                        pltpu.VMEM((2, tq, DV_C), F32)],
        compiler_params=_params(("parallel", "parallel", "parallel", "arbitrary")),
        name="diff_attn",
    )(lam.reshape(1, 1), bias[:, (0, e_hi - e_lo), 0, 0], zv, zv, zv, bias, subln.reshape(1, DV_C))
    return out.reshape(B * S, H_C * DV_C)


def _matmul_res_kernel(a_ref, w_ref, x_ref, o_ref):
    o_ref[...] = x_ref[...] + jnp.dot(a_ref[...], w_ref[...], preferred_element_type=F32)


def matmul_residual(a, w, x, tm):
    T, D = x.shape
    K = a.shape[1]
    tm = min(tm, T)
    assert T % tm == 0
    return pl.pallas_call(
        _matmul_res_kernel,
        grid=(T // tm,),
        in_specs=[pl.BlockSpec((tm, K), lambda i: (i, 0)),
                  pl.BlockSpec((K, D), lambda i: (0, 0)),
                  pl.BlockSpec((tm, D), lambda i: (i, 0))],
        out_specs=pl.BlockSpec((tm, D), lambda i: (i, 0)),
        out_shape=jax.ShapeDtypeStruct((T, D), F32),
        compiler_params=_params(("parallel",)),
        name="matmul_residual",
    )(a, w, x)


def _cross_kernel(x_ref, g_ref, wq_ref, kv_ref, wo_ref, o_ref):
    x = x_ref[0]
    h = _rms(x, g_ref[...]).astype(BF16)
    q = jnp.dot(h, wq_ref[...], preferred_element_type=F32).astype(BF16)
    kv = kv_ref[0]
    hw = H_X * DH_X
    outs = []
    for hd in range(H_X):
        sl = slice(hd * DH_X, (hd + 1) * DH_X)
        s = lax.dot_general(q[:, sl], kv[:, sl], (((1,), (1,)), ((), ())), preferred_element_type=F32)
        m = jnp.max(s, axis=-1, keepdims=True)
        p = jnp.exp(s - m)
        l = jnp.sum(p, axis=-1, keepdims=True)
        vh = kv[:, hw + hd * DH_X: hw + (hd + 1) * DH_X]
        outs.append((jnp.dot(p.astype(BF16), vh, preferred_element_type=F32) / l).astype(BF16))
    o = jnp.concatenate(outs, axis=-1)
    o_ref[0] = x + jnp.dot(o, wo_ref[...], preferred_element_type=F32)


def cross_attention(x, g, wq, kvn, wo, B, S, tm):
    D = x.shape[1]
    tm = min(tm, S)
    assert S % tm == 0
    hw = H_X * DH_X
    out = pl.pallas_call(
        _cross_kernel,
        grid=(B, S // tm),
        in_specs=[pl.BlockSpec((1, tm, D), lambda b, i: (b, i, 0)),
                  pl.BlockSpec((1, D), lambda b, i: (0, 0)),
                  pl.BlockSpec((D, hw), lambda b, i: (0, 0)),
                  pl.BlockSpec((1, N_MEM, 2 * hw), lambda b, i: (b, 0, 0)),
                  pl.BlockSpec((hw, D), lambda b, i: (0, 0))],
        out_specs=pl.BlockSpec((1, tm, D), lambda b, i: (b, i, 0)),
        out_shape=jax.ShapeDtypeStruct((B, S, D), F32),
        compiler_params=_params(("parallel", "parallel")),
        name="cross_attn",
    )(x.reshape(B, S, D), g.reshape(1, D), wq, kvn.reshape(B, N_MEM, 2 * hw), wo)
    return out.reshape(B * S, D)


ROUTE_BIG = 1 << 20


def _router_kernel(x_ref, g_ref, w_ref, b_ref, o_ref, hn_ref):
    h = _rms(x_ref[...], g_ref[...])
    hn_ref[...] = h
    h_hi = h.astype(BF16)
    h_lo = (h - h_hi.astype(F32)).astype(BF16)
    a = jnp.dot(h_hi, w_ref[...], preferred_element_type=F32)
    lg = (a[:, :LANES] + a[:, LANES:] + jnp.dot(h_lo, w_ref[:, :LANES], preferred_element_type=F32)) + b_ref[...]
    lane = lax.broadcasted_iota(jnp.int32, lg.shape, 1)
    first = lambda mask: jnp.min(jnp.where(mask, lane, ROUTE_BIG), axis=-1, keepdims=True)
    is_g = lane < N_GROUPS
    gl = jnp.where(is_g, lg, -jnp.inf)
    gmax = jnp.max(gl, axis=-1, keepdims=True)
    gidx = first(gl == gmax)
    pg = 1.0 / jnp.sum(jnp.where(is_g, jnp.exp(lg - gmax), 0.0), axis=-1, keepdims=True)
    e_id = lane - N_GROUPS
    in_grp = (e_id >= 0) & (e_id < N_EXPERTS) & ((e_id // EXPERTS_PER_GROUP) == gidx)
    el = jnp.where(in_grp, lg, -jnp.inf)
    v0 = jnp.max(el, axis=-1, keepdims=True)
    i0 = first(el == v0)
    el2 = jnp.where(lane == i0, -jnp.inf, el)
    v1 = jnp.max(el2, axis=-1, keepdims=True)
    i1 = first(el2 == v1)
    t = jnp.exp(v1 - v0)
    w0 = pg / (1.0 + t)
    w1 = w0 * t
    res = jnp.where(lane == 0, (i0 - N_GROUPS).astype(F32),
                    jnp.where(lane == 1, (i1 - N_GROUPS).astype(F32),
                              jnp.where(lane == 2, w0, jnp.where(lane == 3, w1, 0.0))))
    o_ref[...] = res


def moe_router(x, g, wg, bg, we, be, tm):
    T, D = x.shape
    tm = min(tm, T)
    wr = jnp.zeros((D, LANES), F32).at[:, :N_GROUPS].set(wg).at[:, N_GROUPS:N_GROUPS + N_EXPERTS].set(we)
    br = jnp.zeros((1, LANES), F32).at[0, :N_GROUPS].set(bg).at[0, N_GROUPS:N_GROUPS + N_EXPERTS].set(be)
    w_hi = wr.astype(BF16)
    wr = jnp.concatenate([w_hi, (wr - w_hi.astype(F32)).astype(BF16)], axis=1)
    return pl.pallas_call(
        _router_kernel,
        grid=(T // tm,),
        in_specs=[pl.BlockSpec((tm, D), lambda i: (i, 0)),
                  pl.BlockSpec((1, D), lambda i: (0, 0)),
                  pl.BlockSpec((D, 2 * LANES), lambda i: (0, 0)),
                  pl.BlockSpec((1, LANES), lambda i: (0, 0))],
        out_specs=[pl.BlockSpec((tm, LANES), lambda i: (i, 0)), pl.BlockSpec((tm, D), lambda i: (i, 0))],
        out_shape=[jax.ShapeDtypeStruct((T, LANES), F32), jax.ShapeDtypeStruct((T, D), F32)],
        compiler_params=_params(("parallel",)),
        name="moe_router",
    )(x, g.reshape(1, D), wr, br)


GATHER_UNROLL = 8


def _start_row_gather(idx_ref, src_hbm, dst, sem, n):
    def body(r, c):
        pltpu.make_async_copy(src_hbm.at[pl.ds(idx_ref[0, 0, r], 1)], dst.at[pl.ds(r, 1)], sem).start()
        return c

    lax.fori_loop(0, n, body, 0, unroll=GATHER_UNROLL)


def _start_row_gather_unrolled(idx_ref, src_hbm, dst, sem, n):
    for r in range(n):
        pltpu.make_async_copy(src_hbm.at[pl.ds(idx_ref[0, 0, r], 1)], dst.at[pl.ds(r, 1)], sem).start()


def _wait_row_gather(src_hbm, dst, sem, n):
    def body(r, c):
        pltpu.make_async_copy(src_hbm.at[pl.ds(0, 1)], dst.at[pl.ds(0, 1)], sem).wait()
        return c

    lax.fori_loop(0, n, body, 0, unroll=GATHER_UNROLL)


def _expert_kernel(be_ref, nu_ref, tok_ref, tok_next_ref, x_hbm, wg_ref, wu_ref, wd_ref, o_ref, xbuf, sem, *, blk):
    i = pl.program_id(0)
    nu = nu_ref[0]
    slot = i % 2

    @pl.when(jnp.logical_and(i == 0, nu > 0))
    def _():
        _start_row_gather(tok_ref, x_hbm, xbuf.at[0], sem.at[0], blk)

    @pl.when(i < nu)
    def _():
        _wait_row_gather(x_hbm, xbuf.at[slot], sem.at[slot], blk)
        _start_row_gather_unrolled(tok_next_ref, x_hbm, xbuf.at[1 - slot], sem.at[1 - slot], blk)
        h = xbuf[slot].astype(BF16)
        a = jnp.dot(h, wg_ref[0], preferred_element_type=F32)
        b = jnp.dot(h, wu_ref[0], preferred_element_type=F32)
        hid = (jax.nn.silu(a) * b).astype(BF16)
        o_ref[...] = jnp.dot(hid, wd_ref[0], preferred_element_type=F32)

    @pl.when(i == nu - 1)
    def _():
        _wait_row_gather(x_hbm, xbuf.at[1 - slot], sem.at[1 - slot], blk)

    @pl.when(i >= nu)
    def _():
        o_ref[...] = jnp.zeros(o_ref.shape, o_ref.dtype)


def moe_experts(x, buf_tok, blk_e, n_used, w_gate, w_up, w_down, blk):
    T, D = x.shape
    cap = buf_tok.shape[0]
    n_blk = cap // blk
    tok = buf_tok.reshape(n_blk, 1, blk)
    grid_spec = pltpu.PrefetchScalarGridSpec(
        num_scalar_prefetch=2,
        grid=(n_blk,),
        in_specs=[pl.BlockSpec((1, 1, blk), lambda i, be, nu: (i, 0, 0), memory_space=pltpu.SMEM),
                  pl.BlockSpec((1, 1, blk), lambda i, be, nu: (jnp.minimum(i + 1, jnp.maximum(nu[0] - 1, 0)), 0, 0),
                               memory_space=pltpu.SMEM),
                  pl.BlockSpec(memory_space=pl.ANY),
                  pl.BlockSpec((1, D, D_FF), lambda i, be, nu: (be[i], 0, 0)),
                  pl.BlockSpec((1, D, D_FF), lambda i, be, nu: (be[i], 0, 0)),
                  pl.BlockSpec((1, D_FF, D), lambda i, be, nu: (be[i], 0, 0))],
        out_specs=pl.BlockSpec((blk, D), lambda i, be, nu: (i, 0)),
        scratch_shapes=[pltpu.VMEM((2, blk, D), F32), pltpu.SemaphoreType.DMA((2,))],
    )
    return pl.pallas_call(
        functools.partial(_expert_kernel, blk=blk),
        grid_spec=grid_spec,
        out_shape=jax.ShapeDtypeStruct((cap, D), F32),
        compiler_params=_params(("arbitrary",)),
        name="moe_experts",
    )(blk_e, n_used, tok, tok, x, w_gate, w_up, w_down)


def _combine_kernel(d_ref, d_next_ref, x_ref, r_ref, y_hbm, g_ref, o_ref, buf, sem, *, tt, final):
    i = pl.program_id(0)
    slot = i % 2

    @pl.when(i == 0)
    def _():
        _start_row_gather(d_ref, y_hbm, buf.at[0], sem.at[0], 2 * tt)

    _wait_row_gather(y_hbm, buf.at[slot], sem.at[slot], 2 * tt)
    _start_row_gather_unrolled(d_next_ref, y_hbm, buf.at[1 - slot], sem.at[1 - slot], 2 * tt)
    r = r_ref[...]
    y = x_ref[...] + r[:, 2:3] * buf[slot, 0:tt, :] + r[:, 3:4] * buf[slot, tt:2 * tt, :]
    o_ref[...] = _rms(y, g_ref[...]) if final else y

    @pl.when(i == pl.num_programs(0) - 1)
    def _():
        _wait_row_gather(y_hbm, buf.at[1 - slot], sem.at[1 - slot], 2 * tt)


def moe_combine(x, routed, dest, y, g_final, tt):
    T, D = x.shape
    n = T // tt
    d = dest.reshape(n, tt, 2).transpose(0, 2, 1).reshape(n, 1, 2 * tt)
    final = g_final is not None
    g = (g_final if final else jnp.ones((D,), F32)).reshape(1, D)
    return pl.pallas_call(
        functools.partial(_combine_kernel, tt=tt, final=final),
        grid=(n,),
        in_specs=[pl.BlockSpec((1, 1, 2 * tt), lambda i: (i, 0, 0), memory_space=pltpu.SMEM),
                  pl.BlockSpec((1, 1, 2 * tt), lambda i: (jnp.minimum(i + 1, n - 1), 0, 0), memory_space=pltpu.SMEM),
                  pl.BlockSpec((tt, D), lambda i: (i, 0)),
                  pl.BlockSpec((tt, LANES), lambda i: (i, 0)),
                  pl.BlockSpec(memory_space=pl.ANY),
                  pl.BlockSpec((1, D), lambda i: (0, 0))],
        out_specs=pl.BlockSpec((tt, D), lambda i: (i, 0)),
        out_shape=jax.ShapeDtypeStruct((T, D), F32),
        scratch_shapes=[pltpu.VMEM((2, 2 * tt, D), F32), pltpu.SemaphoreType.DMA((2,))],
        compiler_params=_params(("arbitrary",)),
        name="moe_combine",
    )(d, d, x, routed, y, g)


def _route_plan(routed, blk):
    T = routed.shape[0]
    e = routed[:, :2].astype(jnp.int32)
    flat_e = e.reshape(-1)
    n_rows = 2 * T
    onehot = (flat_e[:, None] == jnp.arange(N_EXPERTS, dtype=jnp.int32)[None, :]).astype(jnp.int32)
    cum = jnp.cumsum(onehot, axis=0)
    rank = jnp.sum(cum * onehot, axis=1) - 1
    counts = cum[-1]
    padded = (counts + blk - 1) // blk * blk
    pad_end = jnp.cumsum(padded)
    pad_start = pad_end - padded
    dest = pad_start[flat_e] + rank
    n_blk = -(-n_rows // blk) + N_EXPERTS
    cap = n_blk * blk
    flat_t = jnp.repeat(jnp.arange(T, dtype=jnp.int32), 2)
    buf_tok = jnp.zeros((cap,), jnp.int32).at[dest].set(flat_t)
    blk_e = jnp.minimum(jnp.searchsorted(pad_end, jnp.arange(n_blk, dtype=jnp.int32) * blk, side='right'),
                        N_EXPERTS - 1).astype(jnp.int32)
    n_used = (pad_end[-1:] // blk).astype(jnp.int32)
    return buf_tok, dest.reshape(T, 2), blk_e, n_used


def hier_moe(x, g, wg, bg, we, be, w_gate, w_up, w_down, g_final=None, blk=256, tm=512, tt=256):
    routed, hn = moe_router(x, g, wg, bg, we, be, tm)
    buf_tok, dest, blk_e, n_used = _route_plan(routed, blk)
    y = moe_experts(hn, buf_tok, blk_e, n_used, w_gate, w_up, w_down, blk)
    return moe_combine(x, routed, dest, y, g_final, min(tt, x.shape[0]))


def _lambda_init(layer):
    return 0.8 - 0.6 * math.exp(-0.3 * layer)


def _prep_weights(p):
    w = dict(p)
    col = lambda n: jnp.arange(n, dtype=jnp.int32)
    ab = p["ab_w_in"]
    w["ab_w_in"] = (ab * jnp.where(col(ab.shape[-1]) < W_A, DH_A ** -0.5, 1.0)).astype(BF16)
    w["ab_w_out"] = p["ab_w_out"].astype(BF16)
    cw = p["c_w_in"]
    w["c_w_in"] = (cw * jnp.where(col(cw.shape[-1]) < H_C * 2 * DK_C, DK_C ** -0.5 * LOG2E, 1.0)).astype(BF16)
    w["c_w_out"] = p["c_w_out"].astype(BF16)
    w["x_wq"] = (p["x_wq"] * DH_X ** -0.5).astype(BF16)
    w["x_wkv"] = p["x_wkv"].astype(BF16)
    w["x_wo"] = p["x_wo"].astype(BF16)
    for n in ("moe_w_gate", "moe_w_up", "moe_w_down"):
        w[n] = p[n].astype(BF16)
    lv = p["c_lam"].astype(F32)
    w["c_lam_scalar"] = [jnp.exp(jnp.sum(lv[i, 0] * lv[i, 1])) - jnp.exp(jnp.sum(lv[i, 2] * lv[i, 3]))
                         + _lambda_init(2 * i + 1) for i in range(lv.shape[0])]
    w["dil_bias"] = dilated_bias_tiles(p["rel_bias"][:, :H_A])
    w["diff_bias"] = diff_bias_tiles(p["rel_bias"][:, H_A:])
    return w


def _trunk(x3, mem3, w):
    B, S, D = x3.shape
    x = x3.reshape(B * S, D)
    mem = mem3.reshape(B * N_MEM, D)
    for l in range(DEPTH):
        i = l // 2
        if l % 2 == 0:
            zqkv = norm_matmul(x, w["norm_mix"][l], w["ab_w_in"][i], BF16, 1024, 1024, 0, 3 * W_A)
            xg = norm_matmul(x, w["norm_mix"][l], w["ab_w_in"][i], F32, 1024, 1024, 3 * W_A, 2 * D_RNN)
            attn = dilated_attention_fused(zqkv, w["dil_bias"], B, S)
            rec = rg_lru(xg, w["ab_conv_w"][i], w["ab_conv_b"][i], w["lru_wa"][i], w["lru_ba"][i],
                         w["lru_wx"][i], w["lru_bx"][i], w["lru_lam"][i], B, S)
            x = mix_out(attn, rec, w["ab_w_out"][i], x, 512)
        else:
            z = norm_matmul(x, w["norm_mix"][l], w["c_w_in"][i], BF16, 1024, 1024)
            o = diff_attention(z, w["diff_bias"], w["c_lam_scalar"][i], w["c_subln"][i], _lambda_init(l), B, S)
            x = matmul_residual(o, w["c_w_out"][i], x, 512)
        kvn = norm_matmul(mem, w["norm_mem"][l], w["x_wkv"][l], BF16, 1024, 1024)
        x = cross_attention(x, w["norm_cross"][l], w["x_wq"][l], kvn, w["x_wo"][l], B, S, 512)
        x = hier_moe(x, w["norm_ffn"][l], w["moe_wg"][l], w["moe_bg"][l], w["moe_we"][l], w["moe_be"][l],
                     w["moe_w_gate"][l], w["moe_w_up"][l], w["moe_w_down"][l],
                     g_final=w["norm_final"] if l == DEPTH - 1 else None)
    return x.reshape(B, S, D)


def kernel(x_prompt, x_sample, mem_prompt, mem_sample, rel_bias, ab_w_in, ab_conv_w, ab_conv_b, lru_wa, lru_ba, lru_wx, lru_bx, lru_lam, ab_w_out, c_w_in, c_lam, c_subln, c_w_out, norm_mix, norm_cross, norm_mem, x_wq, x_wkv, x_wo, norm_ffn, moe_wg, moe_bg, moe_we, moe_be, moe_w_gate, moe_w_up, moe_w_down, norm_final):
    w = _prep_weights(dict(
        rel_bias=rel_bias, ab_w_in=ab_w_in, ab_conv_w=ab_conv_w, ab_conv_b=ab_conv_b, lru_wa=lru_wa, lru_ba=lru_ba,
        lru_wx=lru_wx, lru_bx=lru_bx, lru_lam=lru_lam, ab_w_out=ab_w_out, c_w_in=c_w_in, c_lam=c_lam,
        c_subln=c_subln, c_w_out=c_w_out, norm_mix=norm_mix, norm_cross=norm_cross, norm_mem=norm_mem, x_wq=x_wq,
        x_wkv=x_wkv, x_wo=x_wo, norm_ffn=norm_ffn, moe_wg=moe_wg, moe_bg=moe_bg, moe_we=moe_we, moe_be=moe_be,
        moe_w_gate=moe_w_gate, moe_w_up=moe_w_up, moe_w_down=moe_w_down, norm_final=norm_final))
    return _trunk(x_prompt, mem_prompt, w), _trunk(x_sample, mem_sample, w)
```

```python
import functools
import math

import jax
import jax.numpy as jnp
from jax import lax
from jax.experimental import pallas as pl
from jax.experimental.pallas import tpu as pltpu

F32 = jnp.float32
BF16 = jnp.bfloat16

D_MODEL = 2048
DEPTH = 4
H_A, DH_A = 16, 64
W_A = H_A * DH_A
DILATED_PATTERNS = ((128, 1), (512, 4), (2048, 16))
D_RNN = 1024
LRU_BLOCKS, LRU_BW = 8, 128
CONV_W = 4
LRU_C = 8.0
H_C, DK_C, DV_C = 8, 128, 256
NUM_BUCKETS, MAX_DISTANCE = 32, 2048
H_X, DH_X, N_MEM = 4, 128, 256
N_GROUPS, EXPERTS_PER_GROUP = 4, 4
N_EXPERTS = N_GROUPS * EXPERTS_PER_GROUP
D_FF = 1024
EPS = 1e-6
NEG = -1e30
LOG2E = math.log2(math.e)

LANES = 128
VMEM_LIMIT = 56 * 1024 * 1024


def _params(sem):
    return pltpu.CompilerParams(dimension_semantics=sem, vmem_limit_bytes=VMEM_LIMIT)


def _rms(x, g):
    return x * lax.rsqrt(jnp.mean(x * x, axis=-1, keepdims=True) + EPS) * g


def _norm_matmul_kernel(x_ref, g_ref, w_ref, o_ref, h_ref):
    j = pl.program_id(1)
    parts = 2 if x_ref.shape[0] % 16 == 0 else 1
    rp = x_ref.shape[0] // parts

    @pl.when(j == 0)
    def _():
        for s in range(parts):
            rows = slice(s * rp, (s + 1) * rp)
            h_ref[rows, :] = _rms(x_ref[rows, :], g_ref[...]).astype(BF16)
            o_ref[rows, :] = jnp.dot(h_ref[rows, :], w_ref[...], preferred_element_type=F32).astype(o_ref.dtype)

    @pl.when(j > 0)
    def _():
        o_ref[...] = jnp.dot(h_ref[...], w_ref[...], preferred_element_type=F32).astype(o_ref.dtype)


def norm_matmul(x, g, w, out_dtype, tm, tn, col0=0, n_cols=None):
    T, D = x.shape
    N = w.shape[1] - col0 if n_cols is None else n_cols
    tm, tn = min(tm, T), min(tn, N)
    assert T % tm == 0 and N % tn == 0 and col0 % tn == 0
    j0 = col0 // tn
    return pl.pallas_call(
        _norm_matmul_kernel,
        grid=(T // tm, N // tn),
        in_specs=[pl.BlockSpec((tm, D), lambda i, j: (i, 0)),
                  pl.BlockSpec((1, D), lambda i, j: (0, 0)),
                  pl.BlockSpec((D, tn), lambda i, j: (0, j0 + j))],
        out_specs=pl.BlockSpec((tm, tn), lambda i, j: (i, j)),
        out_shape=jax.ShapeDtypeStruct((T, N), out_dtype),
        scratch_shapes=[pltpu.VMEM((tm, D), BF16)],
        compiler_params=_params(("parallel", "arbitrary")),
        name="norm_matmul",
    )(x, g.reshape(1, D), w)


def _mix_out_kernel(a1_ref, a2_ref, w_ref, x_ref, o_ref):
    k1 = a1_ref.shape[1]
    acc = jnp.dot(a1_ref[...], w_ref[0:k1, :], preferred_element_type=F32)
    acc += jnp.dot(a2_ref[...], w_ref[k1:, :], preferred_element_type=F32)
    o_ref[...] = x_ref[...] + acc


def mix_out(a1, a2, w, x, tm):
    T, D = x.shape
    K1, K2 = a1.shape[1], a2.shape[1]
    tm = min(tm, T)
    assert T % tm == 0
    return pl.pallas_call(
        _mix_out_kernel,
        grid=(T // tm,),
        in_specs=[pl.BlockSpec((tm, K1), lambda i: (i, 0)),
                  pl.BlockSpec((tm, K2), lambda i: (i, 0)),
                  pl.BlockSpec((K1 + K2, D), lambda i: (0, 0)),
                  pl.BlockSpec((tm, D), lambda i: (i, 0))],
        out_specs=pl.BlockSpec((tm, D), lambda i: (i, 0)),
        out_shape=jax.ShapeDtypeStruct((T, D), F32),
        compiler_params=_params(("parallel",)),
        name="mix_out",
    )(a1, a2, w, x)


def _t5_bucket(rel):
    half = NUM_BUCKETS // 2
    exact = half // 2
    n = jnp.abs(rel)
    large = exact + (jnp.log(jnp.maximum(n, 1).astype(F32) / exact)
                     / math.log(MAX_DISTANCE / exact) * (half - exact)).astype(jnp.int32)
    large = jnp.minimum(large, half - 1)
    return (rel > 0).astype(jnp.int32) * half + jnp.where(n < exact, n, large)


DIL_RADIUS = 64


def _dilated_bias(table, dil, tq):
    win = tq + 2 * DIL_RADIUS
    offs = jnp.array((0, -DIL_RADIUS, -2 * DIL_RADIUS), jnp.int32)
    rel = offs[:, None] - (tq - 1) + jnp.arange(tq + win, dtype=jnp.int32)[None, :]
    w = jnp.where((jnp.abs(rel) <= DIL_RADIUS)[..., None], table[_t5_bucket(rel * dil)].astype(F32), NEG)
    return _toeplitz(w.transpose(0, 2, 1), tq, win)


def _toeplitz_kernel(w_ref, o_ref):
    tq, tk = o_ref.shape[2], o_ref.shape[3]
    x = jnp.broadcast_to(w_ref[0, 0], (tq, w_ref.shape[3]))
    o_ref[0, 0] = pltpu.roll(x, 0, 1, stride=1, stride_axis=0)[:, :tk]


def _toeplitz(w, tq, tk):
    a, b, wd = w.shape
    assert wd == tq + tk and wd % LANES == 0
    w = jnp.roll(w, -(tq - 1), axis=-1).reshape(a, b, 1, wd)
    return pl.pallas_call(
        _toeplitz_kernel,
        grid=(a, b),
        in_specs=[pl.BlockSpec((1, 1, 1, wd), lambda i, j: (i, j, 0, 0))],
        out_specs=pl.BlockSpec((1, 1, tq, tk), lambda i, j: (i, j, 0, 0)),
        out_shape=jax.ShapeDtypeStruct((a, b, tq, tk), F32),
        compiler_params=_params(("parallel", "parallel")),
        name="toeplitz_bias",
    )(w)


PERM_BLOCK = 256


def _residue_perm(dil):
    n_l = PERM_BLOCK // dil
    i = jnp.arange(PERM_BLOCK, dtype=jnp.int32)
    src = (i % n_l) * dil + i // n_l
    return (src[:, None] == i[None, :]).astype(BF16)


def _dilated_fused_kernel(q_ref, k_ref, v_ref, b_ref, p4_ref, p4t_ref, p16_ref, p16t_ref, o_ref,
                          qp, kp, vp, op, lp, osc, lsc, *, S, tq, unroll):
    win = tq + 2 * DIL_RADIUS
    head0 = lax.broadcasted_iota(jnp.int32, (1, LANES), 1) < DH_A

    def attend(qr, kr, vr, g, L, n_sub, out_o, out_l):
        nq = L // tq

        def body(n, carry):
            sub = n // nq
            i = n - sub * nq
            base = sub * L
            i0 = i * tq
            start = jnp.clip(i0 - DIL_RADIUS, 0, L - win)
            var = jnp.where(i == 0, 0, jnp.where(i == nq - 1, 2, 1))
            rows = pl.ds(pl.multiple_of(base + i0, tq), tq)
            krows = pl.ds(pl.multiple_of(base + start, DIL_RADIUS), win)
            q, k, v = qr[rows, :], kr[krows, :], vr[krows, :]
            outs, lses = [], []
            for h in range(2):
                qh = jnp.where(head0 if h == 0 else jnp.logical_not(head0), q, jnp.zeros_like(q))
                s = lax.dot_general(qh, k, (((1,), (1,)), ((), ())), preferred_element_type=F32)
                s = s + b_ref[g, var, h]
                m = jnp.max(s, axis=-1, keepdims=True)
                p = jnp.exp(s - m)
                l = jnp.sum(p, axis=-1, keepdims=True)
                outs.append(jnp.dot(p.astype(BF16), v, preferred_element_type=F32) / l)
                lses.append(m + jnp.log(l))
            out_o[rows, :] = jnp.where(head0, outs[0], outs[1])
            out_l[rows, :] = jnp.where(head0, lses[0], lses[1])
            return carry

        lax.fori_loop(0, n_sub * nq, body, 0, unroll=min(unroll, n_sub * nq))

    def regroup(p_ref, dil):
        n_l, L = PERM_BLOCK // dil, S // dil

        def body(c, carry):
            r0 = pl.multiple_of(c * PERM_BLOCK, PERM_BLOCK)
            x = jnp.concatenate([src[0, pl.ds(r0, PERM_BLOCK), :] for src in (q_ref, k_ref, v_ref)], axis=1)
            y = jnp.dot(p_ref[...], x, preferred_element_type=F32).astype(BF16)
            for j, dst in enumerate((qp, kp, vp)):
                for r in range(dil):
                    dst[pl.ds(pl.multiple_of(r * L + c * n_l, n_l), n_l), :] = (
                        y[r * n_l:(r + 1) * n_l, j * LANES:(j + 1) * LANES])
            return carry

        lax.fori_loop(0, S // PERM_BLOCK, body, 0, unroll=4)

    def merge_back(pt_ref, dil, last):
        n_l, L = PERM_BLOCK // dil, S // dil

        def body(c, carry):
            piece = lambda ref: jnp.concatenate(
                [ref[pl.ds(pl.multiple_of(r * L + c * n_l, n_l), n_l), :] for r in range(dil)], axis=0)
            x = piece(lp)
            x1 = x.astype(BF16)
            x2 = (x - x1.astype(F32)).astype(BF16)
            x3 = (x - x1.astype(F32) - x2.astype(F32)).astype(BF16)
            back = jnp.dot(pt_ref[...], jnp.concatenate([piece(op).astype(BF16), x1, x2, x3], axis=1),
                           preferred_element_type=F32)
            o_g = back[:, :LANES]
            lse_g = back[:, LANES:2 * LANES] + back[:, 2 * LANES:3 * LANES] + back[:, 3 * LANES:]
            rows = pl.ds(pl.multiple_of(c * PERM_BLOCK, PERM_BLOCK), PERM_BLOCK)
            lse0 = lsc[rows, :]
            m = jnp.maximum(lse0, lse_g)
            w0, w1 = jnp.exp(lse0 - m), jnp.exp(lse_g - m)
            den = w0 + w1
            o_new = (w0 * osc[rows, :] + w1 * o_g) / den
            if last:
                o_ref[0, rows, :] = o_new.astype(o_ref.dtype)
            else:
                osc[rows, :] = o_new
                lsc[rows, :] = m + jnp.log(den)
            return carry

        lax.fori_loop(0, S // PERM_BLOCK, body, 0, unroll=4)

    attend(q_ref.at[0], k_ref.at[0], v_ref.at[0], 0, S, 1, osc, lsc)
    regroup(p4_ref, 4)
    attend(qp, kp, vp, 1, S // 4, 4, op, lp)
    merge_back(p4t_ref, 4, False)
    regroup(p16_ref, 16)
    attend(qp, kp, vp, 2, S // 16, 16, op, lp)
    merge_back(p16t_ref, 16, True)


DIL_TQ = 128


def dilated_bias_tiles(table):
    return jnp.stack([_dilated_bias(table, d, DIL_TQ) for _, d in DILATED_PATTERNS])


def dilated_attention_fused(zqkv, bias, B, S, unroll=8):
    tq = DIL_TQ
    dils = tuple(d for _, d in DILATED_PATTERNS)
    assert dils == (1, 4, 16) and S % PERM_BLOCK == 0 and (S // 16) % tq == 0 and S // 16 >= tq + 2 * DIL_RADIUS
    zv = zqkv.reshape(B, S, 3 * W_A)
    hb = W_A // LANES
    win = tq + 2 * DIL_RADIUS
    p4, p16 = _residue_perm(4), _residue_perm(16)
    pspec = pl.BlockSpec((PERM_BLOCK, PERM_BLOCK), lambda b, h: (0, 0))
    out = pl.pallas_call(
        functools.partial(_dilated_fused_kernel, S=S, tq=tq, unroll=unroll),
        grid=(B, hb),
        in_specs=[pl.BlockSpec((1, S, LANES), lambda b, h: (b, 0, h)),
                  pl.BlockSpec((1, S, LANES), lambda b, h: (b, 0, hb + h)),
                  pl.BlockSpec((1, S, LANES), lambda b, h: (b, 0, 2 * hb + h)),
                  pl.BlockSpec((3, 3, 2, tq, win), lambda b, h: (0, 0, h, 0, 0)),
                  pspec, pspec, pspec, pspec],
        out_specs=pl.BlockSpec((1, S, LANES), lambda b, h: (b, 0, h)),
        out_shape=jax.ShapeDtypeStruct((B, S, W_A), BF16),
        scratch_shapes=[pltpu.VMEM((S, LANES), BF16)] * 3 + [pltpu.VMEM((S, LANES), F32)] * 4,
        compiler_params=_params(("parallel", "parallel")),
        name="dilated_attn",
    )(zv, zv, zv, bias, p4, p4.T, p16, p16.T)
    return out.reshape(B * S, W_A)


def _lru_kernel(xr_ref, gr_ref, cw_ref, cb_ref, wa_ref, ba_ref, wx_ref, bx_ref, lc_ref, o_ref, hf_ref, *, S, C, unroll):
    nC = S // C
    row = lax.broadcasted_iota(jnp.int32, (C, LANES), 0)
    E = C + 16

    def conv_chunk(c):
        t0 = pl.multiple_of(c * C, C)
        cur = xr_ref[0, pl.ds(t0, C), :]
        pstart = pl.multiple_of(jnp.maximum(t0 - 8, 0), 8)
        nstart = pl.multiple_of(jnp.minimum(t0 + C, S - 8), 8)
        prev8 = jnp.where(c > 0, xr_ref[0, pl.ds(pstart, 8), :], 0.0)
        next8 = jnp.where(c < nC - 1, xr_ref[0, pl.ds(nstart, 8), :], 0.0)
        ext = jnp.concatenate([prev8, cur, next8], axis=0)
        xc = cb_ref[...] + ext[8:8 + C] * cw_ref[1:2, :]
        xc += pltpu.roll(ext, 1, 0)[8:8 + C] * cw_ref[0:1, :]
        xc += pltpu.roll(ext, E - 1, 0)[8:8 + C] * cw_ref[2:3, :]
        xc += pltpu.roll(ext, E - 2, 0)[8:8 + C] * cw_ref[3:4, :]
        return t0, xc

    def gates(xc, d):
        xb = xc.astype(BF16)
        r = jax.nn.sigmoid(jnp.dot(xb, wa_ref[d], preferred_element_type=F32) + ba_ref[d])
        i = jax.nn.sigmoid(jnp.dot(xb, wx_ref[d], preferred_element_type=F32) + bx_ref[d])
        log_a = lc_ref[d] * r
        a = jnp.exp(log_a)
        u = jnp.sqrt(-jnp.tanh(log_a) * (a * a + 1.0)) * (i * xc)
        return a, u

    def scan(a, u, reverse):
        k = 1
        while k < C:
            shift = C - k if reverse else k
            valid = (row < C - k) if reverse else (row >= k)
            u = u + a * jnp.where(valid, pltpu.roll(u, shift, 0), 0.0)
            a = a * jnp.where(valid, pltpu.roll(a, shift, 0), 1.0)
            k *= 2
        return a, u

    def fwd(c, carry):
        t0, xc = conv_chunk(c)
        a, u = scan(*gates(xc, 0), False)
        h = u + a * carry
        hf_ref[pl.ds(t0, C), :] = h
        return h[C - 1:C, :]

    lax.fori_loop(0, nC, fwd, jnp.zeros((1, LANES), F32), unroll=unroll)

    def bwd(j, carry):
        t0, xc = conv_chunk(nC - 1 - j)
        a, u = scan(*gates(xc, 1), True)
        h = u + a * carry
        g = jax.nn.gelu(gr_ref[0, pl.ds(t0, C), :])
        o_ref[0, pl.ds(t0, C), :] = (g * (hf_ref[pl.ds(t0, C), :] + h)).astype(o_ref.dtype)
        return h[0:1, :]

    lax.fori_loop(0, nC, bwd, jnp.zeros((1, LANES), F32), unroll=unroll)


def rg_lru(xg, conv_w, conv_b, wa, ba, wx, bx, lam, B, S, C=128, unroll=4):
    assert S % C == 0
    xv = xg.reshape(B, S, 2 * D_RNN)
    lc = (-LRU_C * jax.nn.softplus(-lam)).reshape(2, LRU_BLOCKS, 1, LRU_BW)
    vec = lambda t: t.reshape(2, LRU_BLOCKS, 1, LRU_BW)
    wspec = pl.BlockSpec((2, None, LRU_BW, LRU_BW), lambda b, n: (0, n, 0, 0))
    vspec = pl.BlockSpec((2, None, 1, LRU_BW), lambda b, n: (0, n, 0, 0))
    out = pl.pallas_call(
        functools.partial(_lru_kernel, S=S, C=C, unroll=unroll),
        grid=(B, LRU_BLOCKS),
        in_specs=[pl.BlockSpec((1, S, LRU_BW), lambda b, n: (b, 0, n)),
                  pl.BlockSpec((1, S, LRU_BW), lambda b, n: (b, 0, LRU_BLOCKS + n)),
                  pl.BlockSpec((CONV_W, LRU_BW), lambda b, n: (0, n)),
                  pl.BlockSpec((1, LRU_BW), lambda b, n: (0, n)),
                  wspec, vspec, wspec, vspec, vspec],
        out_specs=pl.BlockSpec((1, S, LRU_BW), lambda b, n: (b, 0, n)),
        out_shape=jax.ShapeDtypeStruct((B, S, D_RNN), BF16),
        scratch_shapes=[pltpu.VMEM((S, LRU_BW), F32)],
        compiler_params=_params(("parallel", "parallel")),
        name="rg_lru",
    )(xv, xv, conv_w, conv_b.reshape(1, D_RNN), wa.astype(BF16), vec(ba), wx.astype(BF16), vec(bx), lc)
    return out.reshape(B * S, D_RNN)


BIAS_SATURATION = MAX_DISTANCE // 2 + 1


def _diff_bias(table, tq, tk, e_lo, e_hi):
    rel = (jnp.arange(e_lo, e_hi + 1, dtype=jnp.int32)[:, None] * tk - (tq - 1)
           + jnp.arange(tq + tk, dtype=jnp.int32)[None, :])
    return _toeplitz((table[_t5_bucket(rel)].astype(F32) * LOG2E).transpose(2, 0, 1), tq, tk)


def _lane_tile(x, n):
    return x if n == 1 else jnp.concatenate([x] * n, axis=1)


def _diff_kernel(lam_ref, cb_ref, q_ref, k_ref, v_ref, b_ref, g_ref, o_ref, m_sc, l_sc, acc_sc, *, nk, e_lo, e_hi, mult, rs,
                 unroll):
    kv = pl.program_id(3)
    t = q_ref.shape[1]
    tk = k_ref.shape[1]
    e = kv - (t // tk) * pl.program_id(2)

    @pl.when(kv == 0)
    def _():
        m_sc[...] = jnp.full(m_sc.shape, -jnp.inf, F32)
        l_sc[...] = jnp.zeros(l_sc.shape, F32)
        acc_sc[...] = jnp.zeros(acc_sc.shape, F32)

    def sub(r, carry, *, const_bias):
        r0 = pl.multiple_of(r * rs, rs)
        rows = pl.ds(r0, rs)
        q = q_ref[0, rows, :]
        bias = cb_ref[pl.program_id(1), (e > 0).astype(jnp.int32)] if const_bias else b_ref[0, 0, rows, :]
        for mi in range(2):
            sl = slice(mi * DK_C, (mi + 1) * DK_C)
            s = lax.dot_general(q[:, sl], k_ref[0, :, sl], (((1,), (1,)), ((), ())), preferred_element_type=F32)
            if not const_bias:
                s = s + bias
            m_cur = jnp.max(s, axis=-1, keepdims=True)
            m_prev = m_sc[mi, rows, :]
            m_new = jnp.maximum(m_prev, m_cur + bias if const_bias else m_cur)
            alpha = jnp.exp2(m_prev - m_new)
            p = jnp.exp2(s - _lane_tile(m_new - bias if const_bias else m_new, tk // LANES))
            l_sc[mi, rows, :] = alpha * l_sc[mi, rows, :] + jnp.sum(p, axis=-1, keepdims=True)
            acc_sc[mi, rows, :] = (_lane_tile(alpha, DV_C // LANES) * acc_sc[mi, rows, :]
                                   + jnp.dot(p.astype(BF16), v_ref[0], preferred_element_type=F32))
            m_sc[mi, rows, :] = m_new
        return carry

    far = jnp.logical_or(e >= e_hi, e <= e_lo)
    for const_bias in (False, True):
        @pl.when(far == const_bias)
        def _():
            lax.fori_loop(0, t // rs, functools.partial(sub, const_bias=const_bias), 0,
                          unroll=min(unroll, t // rs))

    @pl.when(kv == nk - 1)
    def _():
        rep = lambda x: _lane_tile(x, DV_C // LANES)
        o = acc_sc[0] / rep(l_sc[0]) - lam_ref[0, 0] * (acc_sc[1] / rep(l_sc[1]))
        o_ref[0] = (_rms(o, g_ref[...]) * mult).astype(o_ref.dtype)


def diff_bias_tiles(table, tq=2048, tk=512):
    e_hi = -(-(BIAS_SATURATION + tq - 1) // tk)
    e_lo = -(-(-(BIAS_SATURATION + tk - 1) // tk))
    return _diff_bias(table, tq, tk, e_lo, e_hi), e_lo, e_hi


def diff_attention(z, bias_tiles, lam, subln, lambda_init, B, S, rs=128, unroll=16):
    bias, e_lo, e_hi = bias_tiles
    tq, tk = bias.shape[2:]
    assert S % tq == 0 and S % tk == 0 and tq % tk == 0 and tq % rs == 0
    u = tq // tk
    zv = z.reshape(B, S, z.shape[1])
    blk = 2 * DK_C
    assert blk == DV_C
    kb, vb = H_C, 2 * H_C
    out = pl.pallas_call(
        functools.partial(_diff_kernel, nk=S // tk, e_lo=e_lo, e_hi=e_hi, mult=1.0 - lambda_init, rs=rs, unroll=unroll),
        grid=(B, H_C, S // tq, S // tk),
        in_specs=[pl.BlockSpec(memory_space=pltpu.SMEM),
                  pl.BlockSpec(memory_space=pltpu.SMEM),
                  pl.BlockSpec((1, tq, blk), lambda b, h, i, j: (b, i, h)),
                  pl.BlockSpec((1, tk, blk), lambda b, h, i, j: (b, j, kb + h)),
                  pl.BlockSpec((1, tk, DV_C), lambda b, h, i, j: (b, j, vb + h)),
                  pl.BlockSpec((1, 1, tq, tk), lambda b, h, i, j: (h, jnp.clip(j - u * i, e_lo, e_hi) - e_lo, 0, 0)),
                  pl.BlockSpec((1, DV_C), lambda b, h, i, j: (0, 0))],
        out_specs=pl.BlockSpec((1, tq, DV_C), lambda b, h, i, j: (b, i, h)),
        out_shape=jax.ShapeDtypeStruct((B, S, H_C * DV_C), BF16),
        scratch_shapes=[pltpu.VMEM((2, tq, LANES), F32), pltpu.VMEM((2, tq, LANES), F32),
                        pltpu.VMEM((2, tq, DV_C), F32)],
        compiler_params=_params(("parallel", "parallel", "parallel", "arbitrary")),
        name="diff_attn",
    )(lam.reshape(1, 1), bias[:, (0, e_hi - e_lo), 0, 0], zv, zv, zv, bias, subln.reshape(1, DV_C))
    return out.reshape(B * S, H_C * DV_C)


def _matmul_res_kernel(a_ref, w_ref, x_ref, o_ref):
    o_ref[...] = x_ref[...] + jnp.dot(a_ref[...], w_ref[...], preferred_element_type=F32)


def matmul_residual(a, w, x, tm):
    T, D = x.shape
    K = a.shape[1]
    tm = min(tm, T)
    assert T % tm == 0
    return pl.pallas_call(
        _matmul_res_kernel,
        grid=(T // tm,),
        in_specs=[pl.BlockSpec((tm, K), lambda i: (i, 0)),
                  pl.BlockSpec((K, D), lambda i: (0, 0)),
                  pl.BlockSpec((tm, D), lambda i: (i, 0))],
        out_specs=pl.BlockSpec((tm, D), lambda i: (i, 0)),
        out_shape=jax.ShapeDtypeStruct((T, D), F32),
        compiler_params=_params(("parallel",)),
        name="matmul_residual",
    )(a, w, x)


def _cross_kernel(x_ref, g_ref, wq_ref, kv_ref, wo_ref, o_ref):
    x = x_ref[0]
    h = _rms(x, g_ref[...]).astype(BF16)
    q = jnp.dot(h, wq_ref[...], preferred_element_type=F32).astype(BF16)
    kv = kv_ref[0]
    hw = H_X * DH_X
    outs = []
    for hd in range(H_X):
        sl = slice(hd * DH_X, (hd + 1) * DH_X)
        s = lax.dot_general(q[:, sl], kv[:, sl], (((1,), (1,)), ((), ())), preferred_element_type=F32)
        m = jnp.max(s, axis=-1, keepdims=True)
        p = jnp.exp(s - m)
        l = jnp.sum(p, axis=-1, keepdims=True)
        vh = kv[:, hw + hd * DH_X: hw + (hd + 1) * DH_X]
        outs.append((jnp.dot(p.astype(BF16), vh, preferred_element_type=F32) / l).astype(BF16))
    o = jnp.concatenate(outs, axis=-1)
    o_ref[0] = x + jnp.dot(o, wo_ref[...], preferred_element_type=F32)


def cross_attention(x, g, wq, kvn, wo, B, S, tm):
    D = x.shape[1]
    tm = min(tm, S)
    assert S % tm == 0
    hw = H_X * DH_X
    out = pl.pallas_call(
        _cross_kernel,
        grid=(B, S // tm),
        in_specs=[pl.BlockSpec((1, tm, D), lambda b, i: (b, i, 0)),
                  pl.BlockSpec((1, D), lambda b, i: (0, 0)),
                  pl.BlockSpec((D, hw), lambda b, i: (0, 0)),
                  pl.BlockSpec((1, N_MEM, 2 * hw), lambda b, i: (b, 0, 0)),
                  pl.BlockSpec((hw, D), lambda b, i: (0, 0))],
        out_specs=pl.BlockSpec((1, tm, D), lambda b, i: (b, i, 0)),
        out_shape=jax.ShapeDtypeStruct((B, S, D), F32),
        compiler_params=_params(("parallel", "parallel")),
        name="cross_attn",
    )(x.reshape(B, S, D), g.reshape(1, D), wq, kvn.reshape(B, N_MEM, 2 * hw), wo)
    return out.reshape(B * S, D)


ROUTE_BIG = 1 << 20


def _router_kernel(x_ref, g_ref, w_ref, b_ref, o_ref, hn_ref):
    h = _rms(x_ref[...], g_ref[...])
    hn_ref[...] = h
    h_hi = h.astype(BF16)
    h_lo = (h - h_hi.astype(F32)).astype(BF16)
    a = jnp.dot(h_hi, w_ref[...], preferred_element_type=F32)
    lg = (a[:, :LANES] + a[:, LANES:] + jnp.dot(h_lo, w_ref[:, :LANES], preferred_element_type=F32)) + b_ref[...]
    lane = lax.broadcasted_iota(jnp.int32, lg.shape, 1)
    first = lambda mask: jnp.min(jnp.where(mask, lane, ROUTE_BIG), axis=-1, keepdims=True)
    is_g = lane < N_GROUPS
    gl = jnp.where(is_g, lg, -jnp.inf)
    gmax = jnp.max(gl, axis=-1, keepdims=True)
    gidx = first(gl == gmax)
    pg = 1.0 / jnp.sum(jnp.where(is_g, jnp.exp(lg - gmax), 0.0), axis=-1, keepdims=True)
    e_id = lane - N_GROUPS
    in_grp = (e_id >= 0) & (e_id < N_EXPERTS) & ((e_id // EXPERTS_PER_GROUP) == gidx)
    el = jnp.where(in_grp, lg, -jnp.inf)
    v0 = jnp.max(el, axis=-1, keepdims=True)
    i0 = first(el == v0)
    el2 = jnp.where(lane == i0, -jnp.inf, el)
    v1 = jnp.max(el2, axis=-1, keepdims=True)
    i1 = first(el2 == v1)
    t = jnp.exp(v1 - v0)
    w0 = pg / (1.0 + t)
    w1 = w0 * t
    res = jnp.where(lane == 0, (i0 - N_GROUPS).astype(F32),
                    jnp.where(lane == 1, (i1 - N_GROUPS).astype(F32),
                              jnp.where(lane == 2, w0, jnp.where(lane == 3, w1, 0.0))))
    o_ref[...] = res


def moe_router(x, g, wg, bg, we, be, tm):
    T, D = x.shape
    tm = min(tm, T)
    wr = jnp.zeros((D, LANES), F32).at[:, :N_GROUPS].set(wg).at[:, N_GROUPS:N_GROUPS + N_EXPERTS].set(we)
    br = jnp.zeros((1, LANES), F32).at[0, :N_GROUPS].set(bg).at[0, N_GROUPS:N_GROUPS + N_EXPERTS].set(be)
    w_hi = wr.astype(BF16)
    wr = jnp.concatenate([w_hi, (wr - w_hi.astype(F32)).astype(BF16)], axis=1)
    return pl.pallas_call(
        _router_kernel,
        grid=(T // tm,),
        in_specs=[pl.BlockSpec((tm, D), lambda i: (i, 0)),
                  pl.BlockSpec((1, D), lambda i: (0, 0)),
                  pl.BlockSpec((D, 2 * LANES), lambda i: (0, 0)),
                  pl.BlockSpec((1, LANES), lambda i: (0, 0))],
        out_specs=[pl.BlockSpec((tm, LANES), lambda i: (i, 0)), pl.BlockSpec((tm, D), lambda i: (i, 0))],
        out_shape=[jax.ShapeDtypeStruct((T, LANES), F32), jax.ShapeDtypeStruct((T, D), F32)],
        compiler_params=_params(("parallel",)),
        name="moe_router",
    )(x, g.reshape(1, D), wr, br)


GATHER_UNROLL = 8


def _start_row_gather(idx_ref, src_hbm, dst, sem, n):
    def body(r, c):
        pltpu.make_async_copy(src_hbm.at[pl.ds(idx_ref[0, 0, r], 1)], dst.at[pl.ds(r, 1)], sem).start()
        return c

    lax.fori_loop(0, n, body, 0, unroll=GATHER_UNROLL)


def _start_row_gather_unrolled(idx_ref, src_hbm, dst, sem, n):
    for r in range(n):
        pltpu.make_async_copy(src_hbm.at[pl.ds(idx_ref[0, 0, r], 1)], dst.at[pl.ds(r, 1)], sem).start()


def _wait_row_gather(src_hbm, dst, sem, n):
    def body(r, c):
        pltpu.make_async_copy(src_hbm.at[pl.ds(0, 1)], dst.at[pl.ds(0, 1)], sem).wait()
        return c

    lax.fori_loop(0, n, body, 0, unroll=GATHER_UNROLL)


def _expert_kernel(be_ref, nu_ref, tok_ref, tok_next_ref, x_hbm, wg_ref, wu_ref, wd_ref, o_ref, xbuf, sem, *, blk):
    i = pl.program_id(0)
    nu = nu_ref[0]
    slot = i % 2

    @pl.when(jnp.logical_and(i == 0, nu > 0))
    def _():
        _start_row_gather(tok_ref, x_hbm, xbuf.at[0], sem.at[0], blk)

    @pl.when(i < nu)
    def _():
        _wait_row_gather(x_hbm, xbuf.at[slot], sem.at[slot], blk)
        _start_row_gather_unrolled(tok_next_ref, x_hbm, xbuf.at[1 - slot], sem.at[1 - slot], blk)
        h = xbuf[slot].astype(BF16)
        a = jnp.dot(h, wg_ref[0], preferred_element_type=F32)
        b = jnp.dot(h, wu_ref[0], preferred_element_type=F32)
        hid = (jax.nn.silu(a) * b).astype(BF16)
        o_ref[...] = jnp.dot(hid, wd_ref[0], preferred_element_type=F32)

    @pl.when(i == nu - 1)
    def _():
        _wait_row_gather(x_hbm, xbuf.at[1 - slot], sem.at[1 - slot], blk)

    @pl.when(i >= nu)
    def _():
        o_ref[...] = jnp.zeros(o_ref.shape, o_ref.dtype)


def moe_experts(x, buf_tok, blk_e, n_used, w_gate, w_up, w_down, blk):
    T, D = x.shape
    cap = buf_tok.shape[0]
    n_blk = cap // blk
    tok = buf_tok.reshape(n_blk, 1, blk)
    grid_spec = pltpu.PrefetchScalarGridSpec(
        num_scalar_prefetch=2,
        grid=(n_blk,),
        in_specs=[pl.BlockSpec((1, 1, blk), lambda i, be, nu: (i, 0, 0), memory_space=pltpu.SMEM),
                  pl.BlockSpec((1, 1, blk), lambda i, be, nu: (jnp.minimum(i + 1, jnp.maximum(nu[0] - 1, 0)), 0, 0),
                               memory_space=pltpu.SMEM),
                  pl.BlockSpec(memory_space=pl.ANY),
                  pl.BlockSpec((1, D, D_FF), lambda i, be, nu: (be[i], 0, 0)),
                  pl.BlockSpec((1, D, D_FF), lambda i, be, nu: (be[i], 0, 0)),
                  pl.BlockSpec((1, D_FF, D), lambda i, be, nu: (be[i], 0, 0))],
        out_specs=pl.BlockSpec((blk, D), lambda i, be, nu: (i, 0)),
        scratch_shapes=[pltpu.VMEM((2, blk, D), F32), pltpu.SemaphoreType.DMA((2,))],
    )
    return pl.pallas_call(
        functools.partial(_expert_kernel, blk=blk),
        grid_spec=grid_spec,
        out_shape=jax.ShapeDtypeStruct((cap, D), F32),
        compiler_params=_params(("arbitrary",)),
        name="moe_experts",
    )(blk_e, n_used, tok, tok, x, w_gate, w_up, w_down)


def _combine_kernel(d_ref, d_next_ref, x_ref, r_ref, y_hbm, g_ref, o_ref, buf, sem, *, tt, final):
    i = pl.program_id(0)
    slot = i % 2

    @pl.when(i == 0)
    def _():
        _start_row_gather(d_ref, y_hbm, buf.at[0], sem.at[0], 2 * tt)

    _wait_row_gather(y_hbm, buf.at[slot], sem.at[slot], 2 * tt)
    _start_row_gather_unrolled(d_next_ref, y_hbm, buf.at[1 - slot], sem.at[1 - slot], 2 * tt)
    r = r_ref[...]
    y = x_ref[...] + r[:, 2:3] * buf[slot, 0:tt, :] + r[:, 3:4] * buf[slot, tt:2 * tt, :]
    o_ref[...] = _rms(y, g_ref[...]) if final else y

    @pl.when(i == pl.num_programs(0) - 1)
    def _():
        _wait_row_gather(y_hbm, buf.at[1 - slot], sem.at[1 - slot], 2 * tt)


def moe_combine(x, routed, dest, y, g_final, tt):
    T, D = x.shape
    n = T // tt
    d = dest.reshape(n, tt, 2).transpose(0, 2, 1).reshape(n, 1, 2 * tt)
    final = g_final is not None
    g = (g_final if final else jnp.ones((D,), F32)).reshape(1, D)
    return pl.pallas_call(
        functools.partial(_combine_kernel, tt=tt, final=final),
        grid=(n,),
        in_specs=[pl.BlockSpec((1, 1, 2 * tt), lambda i: (i, 0, 0), memory_space=pltpu.SMEM),
                  pl.BlockSpec((1, 1, 2 * tt), lambda i: (jnp.minimum(i + 1, n - 1), 0, 0), memory_space=pltpu.SMEM),
                  pl.BlockSpec((tt, D), lambda i: (i, 0)),
                  pl.BlockSpec((tt, LANES), lambda i: (i, 0)),
                  pl.BlockSpec(memory_space=pl.ANY),
                  pl.BlockSpec((1, D), lambda i: (0, 0))],
        out_specs=pl.BlockSpec((tt, D), lambda i: (i, 0)),
        out_shape=jax.ShapeDtypeStruct((T, D), F32),
        scratch_shapes=[pltpu.VMEM((2, 2 * tt, D), F32), pltpu.SemaphoreType.DMA((2,))],
        compiler_params=_params(("arbitrary",)),
        name="moe_combine",
    )(d, d, x, routed, y, g)


def _route_plan(routed, blk):
    T = routed.shape[0]
    e = routed[:, :2].astype(jnp.int32)
    flat_e = e.reshape(-1)
    n_rows = 2 * T
    onehot = (flat_e[:, None] == jnp.arange(N_EXPERTS, dtype=jnp.int32)[None, :]).astype(jnp.int32)
    cum = jnp.cumsum(onehot, axis=0)
    rank = jnp.sum(cum * onehot, axis=1) - 1
    counts = cum[-1]
    padded = (counts + blk - 1) // blk * blk
    pad_end = jnp.cumsum(padded)
    pad_start = pad_end - padded
    dest = pad_start[flat_e] + rank
    n_blk = -(-n_rows // blk) + N_EXPERTS
    cap = n_blk * blk
    flat_t = jnp.repeat(jnp.arange(T, dtype=jnp.int32), 2)
    buf_tok = jnp.zeros((cap,), jnp.int32).at[dest].set(flat_t)
    blk_e = jnp.minimum(jnp.searchsorted(pad_end, jnp.arange(n_blk, dtype=jnp.int32) * blk, side='right'),
                        N_EXPERTS - 1).astype(jnp.int32)
    n_used = (pad_end[-1:] // blk).astype(jnp.int32)
    return buf_tok, dest.reshape(T, 2), blk_e, n_used


def hier_moe(x, g, wg, bg, we, be, w_gate, w_up, w_down, g_final=None, blk=256, tm=512, tt=256):
    routed, hn = moe_router(x, g, wg, bg, we, be, tm)
    buf_tok, dest, blk_e, n_used = _route_plan(routed, blk)
    y = moe_experts(hn, buf_tok, blk_e, n_used, w_gate, w_up, w_down, blk)
    return moe_combine(x, routed, dest, y, g_final, min(tt, x.shape[0]))


def _lambda_init(layer):
    return 0.8 - 0.6 * math.exp(-0.3 * layer)


def _prep_weights(p):
    w = dict(p)
    col = lambda n: jnp.arange(n, dtype=jnp.int32)
    ab = p["ab_w_in"]
    w["ab_w_in"] = (ab * jnp.where(col(ab.shape[-1]) < W_A, DH_A ** -0.5, 1.0)).astype(BF16)
    w["ab_w_out"] = p["ab_w_out"].astype(BF16)
    cw = p["c_w_in"]
    w["c_w_in"] = (cw * jnp.where(col(cw.shape[-1]) < H_C * 2 * DK_C, DK_C ** -0.5 * LOG2E, 1.0)).astype(BF16)
    w["c_w_out"] = p["c_w_out"].astype(BF16)
    w["x_wq"] = (p["x_wq"] * DH_X ** -0.5).astype(BF16)
    w["x_wkv"] = p["x_wkv"].astype(BF16)
    w["x_wo"] = p["x_wo"].astype(BF16)
    for n in ("moe_w_gate", "moe_w_up", "moe_w_down"):
        w[n] = p[n].astype(BF16)
    lv = p["c_lam"].astype(F32)
    w["c_lam_scalar"] = [jnp.exp(jnp.sum(lv[i, 0] * lv[i, 1])) - jnp.exp(jnp.sum(lv[i, 2] * lv[i, 3]))
                         + _lambda_init(2 * i + 1) for i in range(lv.shape[0])]
    w["dil_bias"] = dilated_bias_tiles(p["rel_bias"][:, :H_A])
    w["diff_bias"] = diff_bias_tiles(p["rel_bias"][:, H_A:])
    return w


def _trunk(x3, mem3, w):
    B, S, D = x3.shape
    x = x3.reshape(B * S, D)
    mem = mem3.reshape(B * N_MEM, D)
    for l in range(DEPTH):
        i = l // 2
        if l % 2 == 0:
            zqkv = norm_matmul(x, w["norm_mix"][l], w["ab_w_in"][i], BF16, 1024, 1024, 0, 3 * W_A)
            xg = norm_matmul(x, w["norm_mix"][l], w["ab_w_in"][i], F32, 1024, 1024, 3 * W_A, 2 * D_RNN)
            attn = dilated_attention_fused(zqkv, w["dil_bias"], B, S)
            rec = rg_lru(xg, w["ab_conv_w"][i], w["ab_conv_b"][i], w["lru_wa"][i], w["lru_ba"][i],
                         w["lru_wx"][i], w["lru_bx"][i], w["lru_lam"][i], B, S)
            x = mix_out(attn, rec, w["ab_w_out"][i], x, 512)
        else:
            z = norm_matmul(x, w["norm_mix"][l], w["c_w_in"][i], BF16, 1024, 1024)
            o = diff_attention(z, w["diff_bias"], w["c_lam_scalar"][i], w["c_subln"][i], _lambda_init(l), B, S)
            x = matmul_residual(o, w["c_w_out"][i], x, 512)
        kvn = norm_matmul(mem, w["norm_mem"][l], w["x_wkv"][l], BF16, 1024, 1024)
        x = cross_attention(x, w["norm_cross"][l], w["x_wq"][l], kvn, w["x_wo"][l], B, S, 512)
        x = hier_moe(x, w["norm_ffn"][l], w["moe_wg"][l], w["moe_bg"][l], w["moe_we"][l], w["moe_be"][l],
                     w["moe_w_gate"][l], w["moe_w_up"][l], w["moe_w_down"][l],
                     g_final=w["norm_final"] if l == DEPTH - 1 else None)
    return x.reshape(B, S, D)


def kernel(x_prompt, x_sample, mem_prompt, mem_sample, rel_bias, ab_w_in, ab_conv_w, ab_conv_b, lru_wa, lru_ba, lru_wx, lru_bx, lru_lam, ab_w_out, c_w_in, c_lam, c_subln, c_w_out, norm_mix, norm_cross, norm_mem, x_wq, x_wkv, x_wo, norm_ffn, moe_wg, moe_bg, moe_we, moe_be, moe_w_gate, moe_w_up, moe_w_down, norm_final):
    w = _prep_weights(dict(
        rel_bias=rel_bias, ab_w_in=ab_w_in, ab_conv_w=ab_conv_w, ab_conv_b=ab_conv_b, lru_wa=lru_wa, lru_ba=lru_ba,
        lru_wx=lru_wx, lru_bx=lru_bx, lru_lam=lru_lam, ab_w_out=ab_w_out, c_w_in=c_w_in, c_lam=c_lam,
        c_subln=c_subln, c_w_out=c_w_out, norm_mix=norm_mix, norm_cross=norm_cross, norm_mem=norm_mem, x_wq=x_wq,
        x_wkv=x_wkv, x_wo=x_wo, norm_ffn=norm_ffn, moe_wg=moe_wg, moe_bg=moe_bg, moe_we=moe_we, moe_be=moe_be,
        moe_w_gate=moe_w_gate, moe_w_up=moe_w_up, moe_w_down=moe_w_down, norm_final=norm_final))
    return _trunk(x_prompt, mem_prompt, w), _trunk(x_sample, mem_sample, w)
```
